```python
import jax, jax.numpy as jnp
from jax import lax
import numpy as np

D_MODEL = 1024
BATCH = 8
SEQ = 2048
DEPTH = 2
DEC_BATCH = 128
DEC_SEQ = 4
PAST_LEN = 16384
PAGE_SIZE = 128

C_A = D_MODEL
C_B = (3 * D_MODEL) // 2
CONV_WIDTH = 31
CHUNK = 128
N_GROUPS = 8
GROUP_DIM = C_B // N_GROUPS
D_FF = 4 * D_MODEL
P_DIM = 256
D_IN = 2 * C_A + 2 * C_B + 2 * D_MODEL
EPS = 1e-6

kernel_name = "gated_conformer_gmlp_decoder_step"


def _rmsnorm(x, g):
    xf = x.astype(jnp.float32)
    y = xf * lax.rsqrt(jnp.mean(xf * xf, axis=-1, keepdims=True) + EPS)
    return (y * g.astype(jnp.float32)).astype(x.dtype)


def _layernorm(x, g, b):
    xf = x.astype(jnp.float32)
    mu = jnp.mean(xf, axis=-1, keepdims=True)
    xc = xf - mu
    y = xc * lax.rsqrt(jnp.mean(xc * xc, axis=-1, keepdims=True) + EPS)
    return (y * g.astype(jnp.float32) + b.astype(jnp.float32)).astype(x.dtype)


def _depthwise_causal_conv(x_full, w, b):
    y = lax.conv_general_dilated(
        x_full, w[:, None, :], window_strides=(1,), padding='VALID',
        dimension_numbers=('NWC', 'WIO', 'NWC'), feature_group_count=x_full.shape[-1])
    return y + b


def _chunk_spatial_mix(v, w_s, b_s):
    bsz, t_len, _ = v.shape
    n_chunks = -(-t_len // CHUNK)
    pad = n_chunks * CHUNK - t_len
    vp = jnp.pad(v, ((0, 0), (0, pad), (0, 0))).reshape(bsz, n_chunks, CHUNK, N_GROUPS, GROUP_DIM)
    causal = jnp.tril(jnp.ones((CHUNK, CHUNK), dtype=bool))
    wm = jnp.where(causal[None], w_s, jnp.zeros_like(w_s))
    mixed = jnp.einsum('gts,bnsgc->bntgc', wm, vp) + jnp.transpose(b_s)[None, None, :, :, None]
    return mixed.reshape(bsz, n_chunks * CHUNK, C_B)[:, :t_len]


def _layer(x, conv_prev, p_l, g_mix, w_in, b_in, conv_w, conv_b, ln_a_g, ln_a_b, w_a_out,
           ln_v_g, ln_v_b, w_s, b_s, w_b_out, w_o, g_ffn, w_ff1, w_ff2,
           g_ple, w_ple, w_ple_gate, b_ple_gate):
    h = _rmsnorm(x, g_mix)
    z = h @ w_in + b_in
    a_lin = z[..., :C_A]
    a_gate = z[..., C_A:2 * C_A]
    zb = z[..., 2 * C_A:2 * C_A + 2 * C_B]
    g_a = z[..., 2 * C_A + 2 * C_B:2 * C_A + 2 * C_B + D_MODEL]
    g_b = z[..., 2 * C_A + 2 * C_B + D_MODEL:]
    glu = a_lin * jax.nn.sigmoid(a_gate)
    conv_in = jnp.concatenate([conv_prev, glu], axis=1)
    ya = _depthwise_causal_conv(conv_in, conv_w, conv_b)
    ya = jax.nn.silu(_layernorm(ya, ln_a_g, ln_a_b)) @ w_a_out
    new_conv = conv_in[:, -(CONV_WIDTH - 1):]
    zb = jax.nn.gelu(zb)
    u = zb[..., :C_B]
    v = _layernorm(zb[..., C_B:], ln_v_g, ln_v_b)
    yb = (u * _chunk_spatial_mix(v, w_s, b_s)) @ w_b_out
    m = jax.nn.sigmoid(g_a) * ya + jax.nn.sigmoid(g_b) * yb
    x = x + m @ w_o
    hf = _rmsnorm(x, g_ffn)
    x = x + jnp.square(jax.nn.relu(hf @ w_ff1)) @ w_ff2
    gate = jax.nn.sigmoid(_rmsnorm(x, g_ple) @ w_ple_gate + b_ple_gate)
    x = x + gate * (p_l @ w_ple)
    t_len = x.shape[1]
    start = ((t_len - 1) // CHUNK) * CHUNK
    return x, new_conv, v[:, start:]


def setup_inputs(seed: int = 0) -> dict:
    key = jax.random.key(seed)
    ks = jax.random.split(key, 32)

    def nrm(k, shape, scale):
        return (jax.random.normal(k, shape, dtype=jnp.float32) * scale).astype(jnp.float32)

    def gain(k, shape):
        return 1.0 + nrm(k, shape, 0.05)

    L = DEPTH
    return {
        "x_prompt": nrm(ks[0], (BATCH, SEQ, D_MODEL), 1.0),
        "x_sample": nrm(ks[1], (DEC_BATCH, DEC_SEQ, D_MODEL), 1.0),
        "state_conv": nrm(ks[2], (L, DEC_BATCH, CONV_WIDTH - 1, C_A), 0.5),
        "p_prompt": nrm(ks[3], (L, BATCH, SEQ, P_DIM), 1.0),
        "p_sample": nrm(ks[4], (L, DEC_BATCH, DEC_SEQ, P_DIM), 1.0),
        "g_mix": gain(ks[5], (L, D_MODEL)),
        "w_in": nrm(ks[6], (L, D_MODEL, D_IN), D_MODEL ** -0.5),
        "b_in": nrm(ks[7], (L, D_IN), 0.02),
        "conv_w": nrm(ks[8], (L, CONV_WIDTH, C_A), CONV_WIDTH ** -0.5),
        "conv_b": nrm(ks[9], (L, C_A), 0.02),
        "ln_a_g": gain(ks[10], (L, C_A)),
        "ln_a_b": nrm(ks[11], (L, C_A), 0.02),
        "w_a_out": nrm(ks[12], (L, C_A, D_MODEL), C_A ** -0.5),
        "ln_v_g": gain(ks[13], (L, C_B)),
        "ln_v_b": nrm(ks[14], (L, C_B), 0.02),
        "w_s": nrm(ks[15], (L, N_GROUPS, CHUNK, CHUNK), CHUNK ** -0.5),
        "b_s": 1.0 + nrm(ks[16], (L, N_GROUPS, CHUNK), 0.1),
        "w_b_out": nrm(ks[17], (L, C_B, D_MODEL), C_B ** -0.5),
        "w_o": nrm(ks[18], (L, D_MODEL, D_MODEL), D_MODEL ** -0.5),
        "g_ffn": gain(ks[19], (L, D_MODEL)),
        "w_ff1": nrm(ks[20], (L, D_MODEL, D_FF), D_MODEL ** -0.5),
        "w_ff2": nrm(ks[21], (L, D_FF, D_MODEL), D_FF ** -0.5),
        "g_ple": gain(ks[22], (L, D_MODEL)),
        "w_ple": nrm(ks[23], (L, P_DIM, D_MODEL), P_DIM ** -0.5),
        "w_ple_gate": nrm(ks[24], (L, D_MODEL, D_MODEL), D_MODEL ** -0.5),
        "b_ple_gate": nrm(ks[25], (L, D_MODEL), 0.02),
        "g_final": gain(ks[26], (D_MODEL,)),
    }


def reference(x_prompt, x_sample, state_conv, p_prompt, p_sample,
              g_mix, w_in, b_in, conv_w, conv_b, ln_a_g, ln_a_b, w_a_out,
              ln_v_g, ln_v_b, w_s, b_s, w_b_out, w_o, g_ffn, w_ff1, w_ff2,
              g_ple, w_ple, w_ple_gate, b_ple_gate, g_final):
    xp = x_prompt
    xs = x_sample
    conv_p_list, conv_s_list, v_p_list, v_s_list = [], [], [], []
    for i in range(DEPTH):
        w_i = (g_mix[i], w_in[i], b_in[i], conv_w[i], conv_b[i], ln_a_g[i], ln_a_b[i], w_a_out[i],
               ln_v_g[i], ln_v_b[i], w_s[i], b_s[i], w_b_out[i], w_o[i], g_ffn[i], w_ff1[i],
               w_ff2[i], g_ple[i], w_ple[i], w_ple_gate[i], b_ple_gate[i])
        zero_hist = jnp.zeros((xp.shape[0], CONV_WIDTH - 1, C_A), dtype=xp.dtype)
        xp, conv_p, v_p = _layer(xp, zero_hist, p_prompt[i], *w_i)
        xs, conv_s, v_s = _layer(xs, state_conv[i], p_sample[i], *w_i)
        conv_p_list.append(conv_p)
        conv_s_list.append(conv_s)
        v_p_list.append(v_p)
        v_s_list.append(v_s)
    y_prompt = _rmsnorm(xp, g_final)
    y_sample = _rmsnorm(xs, g_final)
    conv_prompt = jnp.stack(conv_p_list, axis=0)
    conv_sample = jnp.stack(conv_s_list, axis=0)
    v_prompt = jnp.stack(v_p_list, axis=0)
    v_sample = jnp.stack(v_s_list, axis=0)
    return (y_prompt, y_sample, conv_prompt, conv_sample, v_prompt, v_sample)
```

```python
import functools

import jax
import jax.numpy as jnp
from jax.experimental import pallas as pl
from jax.experimental.pallas import tpu as pltpu

D_MODEL = 1024
C_A = D_MODEL
C_B = (3 * D_MODEL) // 2
CONV_WIDTH = 31
HIST = CONV_WIDTH - 1
CHUNK = 128
N_GROUPS = 8
GROUP_DIM = C_B // N_GROUPS
PAIR_DIM = 2 * GROUP_DIM
N_PAIRS = N_GROUPS // 2
D_FF = 4 * D_MODEL
P_DIM = 256
EPS = 1e-6

COL_A = 0
COL_B = 2 * C_A
COL_G = 2 * C_A + 2 * C_B
D_IN = 2 * C_A + 2 * C_B + 2 * D_MODEL

HIST_PAD = 32
TILE_T = 256
FFN_ROWS = 512
V7X_VMEM_LIMIT_BYTES = 56 * 1024 * 1024

BF16 = jnp.bfloat16
F32 = jnp.float32


def _rms(x, g):
    return x * jax.lax.rsqrt(jnp.mean(x * x, axis=-1, keepdims=True) + EPS) * g


def _ln(x, g, b):
    mu = jnp.mean(x, axis=-1, keepdims=True)
    xc = x - mu
    return xc * jax.lax.rsqrt(jnp.mean(xc * xc, axis=-1, keepdims=True) + EPS) * g + b


def _dot(a, b):
    return jnp.dot(a.astype(BF16), b, preferred_element_type=F32)


def _const_spec(shape):
    zeros = (0,) * len(shape)
    return pl.BlockSpec(shape, lambda *_: zeros, pipeline_mode=pl.Buffered(1))


def _mixer_prompt_kernel(x_ref, gmix_ref, win_ref, bin_ref, convw_ref, convb_ref, lnag_ref,
                         lnab_ref, waout_ref, lnvg_ref, lnvb_ref, ws_ref, bsp_ref, wbout_ref,
                         wo_ref, x1_ref, conv_ref, vlast_ref, cbuf_ref):
    t = pl.program_id(1)
    n_t = pl.num_programs(1)
    tt = x_ref.shape[0]

    @pl.when(t == 0)
    def _():
        cbuf_ref[0:HIST_PAD, :] = jnp.zeros((HIST_PAD, C_A), F32)

    x = x_ref[...]
    h = _rms(x, gmix_ref[...]).astype(BF16)

    za = jnp.dot(h, win_ref[:, COL_A:COL_A + 2 * C_A], preferred_element_type=F32)
    za = za + bin_ref[:, COL_A:COL_A + 2 * C_A]
    glu = za[:, :C_A] * jax.nn.sigmoid(za[:, C_A:])
    cbuf_ref[HIST_PAD:HIST_PAD + tt, :] = glu
    acc = jnp.zeros((tt, C_A), F32) + convb_ref[...]
    for k in range(CONV_WIDTH):
        off = HIST_PAD - HIST + k
        acc = acc + convw_ref[k:k + 1, :] * cbuf_ref[off:off + tt, :]
    cbuf_ref[0:HIST_PAD, :] = cbuf_ref[tt:tt + HIST_PAD, :]
    ya = _ln(acc, lnag_ref[...], lnab_ref[...])
    ya = _dot(ya * jax.nn.sigmoid(ya), waout_ref[...])

    zb = jnp.dot(h, win_ref[:, COL_B:COL_B + 2 * C_B], preferred_element_type=F32)
    zb = jax.nn.gelu(zb + bin_ref[:, COL_B:COL_B + 2 * C_B])
    u = zb[:, :C_B]
    v = _ln(zb[:, C_B:], lnvg_ref[...], lnvb_ref[...])
    vb = v.astype(BF16)
    row = jax.lax.broadcasted_iota(jnp.int32, (CHUNK, 2 * CHUNK), 0)
    col = jax.lax.broadcasted_iota(jnp.int32, (CHUNK, 2 * CHUNK), 1)
    causal = (col % CHUNK) <= row
    first_group = jax.lax.broadcasted_iota(jnp.int32, (CHUNK, PAIR_DIM), 1) < GROUP_DIM
    zero = jnp.zeros((CHUNK, PAIR_DIM), BF16)
    mixed_chunks = []
    for c in range(tt // CHUNK):
        pieces = []
        for p in range(N_PAIRS):
            wp = jnp.where(causal, ws_ref[p], jnp.zeros_like(ws_ref[p]))
            vs = vb[c * CHUNK:(c + 1) * CHUNK, p * PAIR_DIM:(p + 1) * PAIR_DIM]
            rhs = jnp.concatenate([jnp.where(first_group, vs, zero),
                                   jnp.where(first_group, zero, vs)], axis=0)
            pieces.append(jnp.dot(wp, rhs, preferred_element_type=F32))
        mixed_chunks.append(jnp.concatenate(pieces, axis=1) + bsp_ref[...])
    mixed = jnp.concatenate(mixed_chunks, axis=0)
    yb = _dot(u * mixed, wbout_ref[...])

    zg = jnp.dot(h, win_ref[:, COL_G:COL_G + 2 * D_MODEL], preferred_element_type=F32)
    zg = zg + bin_ref[:, COL_G:COL_G + 2 * D_MODEL]
    m = jax.nn.sigmoid(zg[:, :D_MODEL]) * ya + jax.nn.sigmoid(zg[:, D_MODEL:]) * yb
    x1_ref[...] = x + _dot(m, wo_ref[...])

    @pl.when(t == n_t - 1)
    def _():
        conv_ref[...] = glu[tt - HIST:, :]
        vlast_ref[...] = v[tt - CHUNK:, :]


def _mixer_prompt(x, lw):
    bsz, seq, _ = x.shape
    tt = TILE_T
    grid = (bsz, seq // tt)
    row_spec = pl.BlockSpec((None, tt, D_MODEL), lambda b, t: (b, t, 0))
    in_specs = [row_spec] + [_const_spec(a.shape) for a in lw]
    out_shape = (jax.ShapeDtypeStruct((bsz, seq, D_MODEL), F32),
                 jax.ShapeDtypeStruct((bsz, HIST, C_A), F32),
                 jax.ShapeDtypeStruct((bsz, CHUNK, C_B), F32))
    out_specs = (row_spec,
                 pl.BlockSpec((None, HIST, C_A), lambda b, t: (b, 0, 0)),
                 pl.BlockSpec((None, CHUNK, C_B), lambda b, t: (b, 0, 0)))
    return pl.pallas_call(
        _mixer_prompt_kernel,
        grid=grid,
        in_specs=in_specs,
        out_specs=out_specs,
        out_shape=out_shape,
        scratch_shapes=[pltpu.VMEM((HIST_PAD + tt, C_A), F32)],
        compiler_params=pltpu.CompilerParams(
            dimension_semantics=("arbitrary", "arbitrary"),
            vmem_limit_bytes=V7X_VMEM_LIMIT_BYTES),
        name="mixer_prompt",
    )(x, *lw)


def _mixer_sample_kernel(x_ref, st_ref, gmix_ref, win_ref, bin_ref, convw_ref, convb_ref,
                         lnag_ref, lnab_ref, waout_ref, lnvg_ref, lnvb_ref, wmix_ref, bmix_ref,
                         wbout_ref, wo_ref, x1_ref, glu_ref, v_ref):
    n_s, nb, _ = x_ref.shape
    x = x_ref[...].reshape(n_s * nb, D_MODEL)
    h = _rms(x, gmix_ref[...]).astype(BF16)

    za = jnp.dot(h, win_ref[:, COL_A:COL_A + 2 * C_A], preferred_element_type=F32)
    za = za + bin_ref[:, COL_A:COL_A + 2 * C_A]
    glu = za[:, :C_A] * jax.nn.sigmoid(za[:, C_A:])
    glu_ref[...] = glu.reshape(n_s, nb, C_A)
    conv_rows = []
    for s in range(n_s):
        acc = jnp.zeros((nb, C_A), F32) + convb_ref[...]
        for j in range(s, HIST):
            acc = acc + convw_ref[j - s:j - s + 1, :] * st_ref[j]
        for r in range(s + 1):
            k = HIST - s + r
            acc = acc + convw_ref[k:k + 1, :] * glu[r * nb:(r + 1) * nb, :]
        conv_rows.append(acc)
    ya = _ln(jnp.concatenate(conv_rows, axis=0), lnag_ref[...], lnab_ref[...])
    ya = _dot(ya * jax.nn.sigmoid(ya), waout_ref[...])

    zb = jnp.dot(h, win_ref[:, COL_B:COL_B + 2 * C_B], preferred_element_type=F32)
    zb = jax.nn.gelu(zb + bin_ref[:, COL_B:COL_B + 2 * C_B])
    u = zb[:, :C_B]
    v = _ln(zb[:, C_B:], lnvg_ref[...], lnvb_ref[...])
    v_ref[...] = v.reshape(n_s, nb, C_B)
    mixed_rows = []
    for s in range(n_s):
        acc = jnp.zeros((nb, C_B), F32) + bmix_ref[s:s + 1, :]
        for r in range(s + 1):
            acc = acc + wmix_ref[s, r:r + 1, :] * v[r * nb:(r + 1) * nb, :]
        mixed_rows.append(acc)
    mixed = jnp.concatenate(mixed_rows, axis=0)
    yb = _dot(u * mixed, wbout_ref[...])

    zg = jnp.dot(h, win_ref[:, COL_G:COL_G + 2 * D_MODEL], preferred_element_type=F32)
    zg = zg + bin_ref[:, COL_G:COL_G + 2 * D_MODEL]
    m = jax.nn.sigmoid(zg[:, :D_MODEL]) * ya + jax.nn.sigmoid(zg[:, D_MODEL:]) * yb
    x1_ref[...] = (x + _dot(m, wo_ref[...])).reshape(n_s, nb, D_MODEL)


SAMPLE_SEQS = 64


def _mixer_sample(x_t, st_t, lw):
    n_s, nseq, _ = x_t.shape
    nb = SAMPLE_SEQS
    grid = (nseq // nb,)
    x_spec = pl.BlockSpec((n_s, nb, D_MODEL), lambda i: (0, i, 0))
    in_specs = [x_spec, pl.BlockSpec((HIST, nb, C_A), lambda i: (0, i, 0))]
    in_specs += [_const_spec(a.shape) for a in lw]
    out_shape = (jax.ShapeDtypeStruct((n_s, nseq, D_MODEL), F32),
                 jax.ShapeDtypeStruct((n_s, nseq, C_A), F32),
                 jax.ShapeDtypeStruct((n_s, nseq, C_B), F32))
    out_specs = (x_spec,
                 pl.BlockSpec((n_s, nb, C_A), lambda i: (0, i, 0)),
                 pl.BlockSpec((n_s, nb, C_B), lambda i: (0, i, 0)))
    return pl.pallas_call(
        _mixer_sample_kernel,
        grid=grid,
        in_specs=in_specs,
        out_specs=out_specs,
        out_shape=out_shape,
        compiler_params=pltpu.CompilerParams(
            dimension_semantics=("arbitrary",),
            vmem_limit_bytes=V7X_VMEM_LIMIT_BYTES),
        name="mixer_sample",
    )(x_t, st_t, *lw)


def _ffn_kernel(x_ref, p_ref, gffn_ref, w1_ref, w2_ref, gple_ref, wple_ref, wpg_ref, bpg_ref,
                gfin_ref, y_ref, *, final_norm):
    x = x_ref[...]
    hf = _rms(x, gffn_ref[...])
    a = jnp.maximum(_dot(hf, w1_ref[...]), 0.0)
    x = x + _dot(a * a, w2_ref[...])
    gate = jax.nn.sigmoid(_dot(_rms(x, gple_ref[...]), wpg_ref[...]) + bpg_ref[...])
    x = x + gate * _dot(p_ref[...], wple_ref[...])
    if final_norm:
        x = _rms(x, gfin_ref[...])
    y_ref[...] = x


def _ffn(x2d, p2d, fw, final_norm):
    rows = x2d.shape[0]
    tr = min(FFN_ROWS, rows)
    grid = (rows // tr,)
    in_specs = [pl.BlockSpec((tr, D_MODEL), lambda i: (i, 0)),
                pl.BlockSpec((tr, P_DIM), lambda i: (i, 0))]
    in_specs += [_const_spec(a.shape) for a in fw]
    return pl.pallas_call(
        functools.partial(_ffn_kernel, final_norm=final_norm),
        grid=grid,
        in_specs=in_specs,
        out_specs=pl.BlockSpec((tr, D_MODEL), lambda i: (i, 0)),
        out_shape=jax.ShapeDtypeStruct((rows, D_MODEL), F32),
        compiler_params=pltpu.CompilerParams(
            dimension_semantics=("arbitrary",),
            vmem_limit_bytes=V7X_VMEM_LIMIT_BYTES),
        name="ffn",
    )(x2d, p2d, *fw)


def _row(a):
    return a.reshape(1, -1)


def kernel(x_prompt, x_sample, state_conv, p_prompt, p_sample, g_mix, w_in, b_in, conv_w, conv_b,
           ln_a_g, ln_a_b, w_a_out, ln_v_g, ln_v_b, w_s, b_s, w_b_out, w_o, g_ffn, w_ff1, w_ff2,
           g_ple, w_ple, w_ple_gate, b_ple_gate, g_final):
    depth = w_in.shape[0]
    bsz, seq, _ = x_prompt.shape
    nseq, n_s, _ = x_sample.shape

    xp = x_prompt
    xs_t = jnp.transpose(x_sample, (1, 0, 2))
    conv_p, conv_s, v_p, v_s = [], [], [], []
    for i in range(depth):
        shared = (_row(g_mix[i]), w_in[i].astype(BF16), _row(b_in[i]), conv_w[i], _row(conv_b[i]),
                  _row(ln_a_g[i]), _row(ln_a_b[i]), w_a_out[i].astype(BF16),
                  _row(ln_v_g[i]), _row(ln_v_b[i]))
        tail = (w_b_out[i].astype(BF16), w_o[i].astype(BF16))
        ws_pairs = jnp.concatenate([w_s[i, 0::2], w_s[i, 1::2]], axis=2).astype(BF16)
        bs_full = jnp.repeat(jnp.transpose(b_s[i]), GROUP_DIM, axis=1)
        wmix = jnp.repeat(jnp.transpose(w_s[i, :, :n_s, :n_s], (1, 2, 0)), GROUP_DIM, axis=2)
        fw = (_row(g_ffn[i]), w_ff1[i].astype(BF16), w_ff2[i].astype(BF16), _row(g_ple[i]),
              w_ple[i].astype(BF16), w_ple_gate[i].astype(BF16), _row(b_ple_gate[i]),
              _row(g_final))
        last = i == depth - 1

        xp, cp, vp = _mixer_prompt(xp, shared + (ws_pairs, bs_full) + tail)
        xp = _ffn(xp.reshape(bsz * seq, D_MODEL), p_prompt[i].reshape(bsz * seq, P_DIM), fw,
                  last).reshape(bsz, seq, D_MODEL)

        st_t = jnp.transpose(state_conv[i], (1, 0, 2))
        xs_t, glu_t, v_t = _mixer_sample(xs_t, st_t, shared + (wmix, bs_full[:n_s]) + tail)
        p_t = jnp.transpose(p_sample[i], (1, 0, 2)).reshape(n_s * nseq, P_DIM)
        xs_t = _ffn(xs_t.reshape(n_s * nseq, D_MODEL), p_t, fw, last).reshape(n_s, nseq, D_MODEL)

        conv_p.append(cp)
        v_p.append(vp)
        conv_s.append(jnp.concatenate([state_conv[i][:, n_s:], jnp.transpose(glu_t, (1, 0, 2))],
                                      axis=1))
        v_s.append(jnp.transpose(v_t, (1, 0, 2)))

    y_sample = jnp.transpose(xs_t, (1, 0, 2))
    return (xp, y_sample, jnp.stack(conv_p), jnp.stack(conv_s), jnp.stack(v_p), jnp.stack(v_s))
```

```python
import functools

import jax
import jax.numpy as jnp
from jax.experimental import pallas as pl
from jax.experimental.pallas import tpu as pltpu

D_MODEL = 1024
C_A = D_MODEL
C_B = (3 * D_MODEL) // 2
CONV_WIDTH = 31
HIST = CONV_WIDTH - 1
CHUNK = 128
N_GROUPS = 8
GROUP_DIM = C_B // N_GROUPS
PAIR_DIM = 2 * GROUP_DIM
N_PAIRS = N_GROUPS // 2
D_FF = 4 * D_MODEL
P_DIM = 256
EPS = 1e-6

COL_A = 0
COL_B = 2 * C_A
COL_G = 2 * C_A + 2 * C_B
D_IN = 2 * C_A + 2 * C_B + 2 * D_MODEL

LANES = 128
HIST_PAD = 32
TILE_T = 256
FFN_ROWS = 512
V7X_VMEM_LIMIT_BYTES = 56 * 1024 * 1024

BF16 = jnp.bfloat16
F32 = jnp.float32


def _rms(x, g):
    return x * jax.lax.rsqrt(jnp.mean(x * x, axis=-1, keepdims=True) + EPS) * g


def _ln(x, g, b):
    mu = jnp.mean(x, axis=-1, keepdims=True)
    xc = x - mu
    return xc * jax.lax.rsqrt(jnp.mean(xc * xc, axis=-1, keepdims=True) + EPS) * g + b


def _dot(a, b):
    return jnp.dot(a.astype(BF16), b, preferred_element_type=F32)


def _const_spec(shape):
    zeros = (0,) * len(shape)
    return pl.BlockSpec(shape, lambda *_: zeros, pipeline_mode=pl.Buffered(1))


def _mixer_prompt_kernel(x_ref, gmix_ref, win_ref, bin_ref, convw_ref, convb_ref, lnag_ref,
                         lnab_ref, waout_ref, lnvg_ref, lnvb_ref, ws_ref, bsp_ref, wbout_ref,
                         wo_ref, x1_ref, conv_ref, vlast_ref, cbuf_ref):
    t = pl.program_id(1)
    n_t = pl.num_programs(1)
    tt = x_ref.shape[0]

    @pl.when(t == 0)
    def _():
        cbuf_ref[:, 0:HIST_PAD, :] = jnp.zeros((C_A // LANES, HIST_PAD, LANES), F32)

    x = x_ref[...]
    h = _rms(x, gmix_ref[...]).astype(BF16)

    za = jnp.dot(h, win_ref[:, COL_A:COL_A + 2 * C_A], preferred_element_type=F32)
    za = za + bin_ref[:, COL_A:COL_A + 2 * C_A]
    glu = za[:, :C_A] * jax.nn.sigmoid(za[:, C_A:])
    strips = []
    for l in range(C_A // LANES):
        lanes = slice(l * LANES, (l + 1) * LANES)
        cbuf_ref[l, HIST_PAD:HIST_PAD + tt, :] = glu[:, lanes]
        acc = jnp.zeros((tt, LANES), F32) + convb_ref[:, lanes]
        for k in range(CONV_WIDTH):
            off = HIST_PAD - HIST + k
            acc = acc + convw_ref[k:k + 1, lanes] * cbuf_ref[l, off:off + tt, :]
        cbuf_ref[l, 0:HIST_PAD, :] = cbuf_ref[l, tt:tt + HIST_PAD, :]
        strips.append(acc)
    ya = _ln(jnp.concatenate(strips, axis=1), lnag_ref[...], lnab_ref[...])
    ya = _dot(ya * jax.nn.sigmoid(ya), waout_ref[...])

    zb = jnp.dot(h, win_ref[:, COL_B:COL_B + 2 * C_B], preferred_element_type=F32)
    zb = jax.nn.gelu(zb + bin_ref[:, COL_B:COL_B + 2 * C_B])
    u = zb[:, :C_B]
    v = _ln(zb[:, C_B:], lnvg_ref[...], lnvb_ref[...])
    vb = v.astype(BF16)
    row = jax.lax.broadcasted_iota(jnp.int32, (CHUNK, 2 * CHUNK), 0)
    col = jax.lax.broadcasted_iota(jnp.int32, (CHUNK, 2 * CHUNK), 1)
    causal = (col % CHUNK) <= row
    first_group = jax.lax.broadcasted_iota(jnp.int32, (CHUNK, PAIR_DIM), 1) < GROUP_DIM
    zero = jnp.zeros((CHUNK, PAIR_DIM), BF16)
    mixed_chunks = []
    for c in range(tt // CHUNK):
        pieces = []
        for p in range(N_PAIRS):
            wp = jnp.where(causal, ws_ref[p], jnp.zeros_like(ws_ref[p]))
            vs = vb[c * CHUNK:(c + 1) * CHUNK, p * PAIR_DIM:(p + 1) * PAIR_DIM]
            rhs = jnp.concatenate([jnp.where(first_group, vs, zero),
                                   jnp.where(first_group, zero, vs)], axis=0)
            pieces.append(jnp.dot(wp, rhs, preferred_element_type=F32))
        mixed_chunks.append(jnp.concatenate(pieces, axis=1) + bsp_ref[...])
    mixed = jnp.concatenate(mixed_chunks, axis=0)
    yb = _dot(u * mixed, wbout_ref[...])

    zg = jnp.dot(h, win_ref[:, COL_G:COL_G + 2 * D_MODEL], preferred_element_type=F32)
    zg = zg + bin_ref[:, COL_G:COL_G + 2 * D_MODEL]
    m = jax.nn.sigmoid(zg[:, :D_MODEL]) * ya + jax.nn.sigmoid(zg[:, D_MODEL:]) * yb
    x1_ref[...] = x + _dot(m, wo_ref[...])

    @pl.when(t == n_t - 1)
    def _():
        conv_ref[...] = glu[tt - HIST:, :]
        vlast_ref[...] = v[tt - CHUNK:, :]


def _mixer_prompt(x, lw):
    bsz, seq, _ = x.shape
    tt = TILE_T
    grid = (bsz, seq // tt)
    row_spec = pl.BlockSpec((None, tt, D_MODEL), lambda b, t: (b, t, 0))
    in_specs = [row_spec] + [_const_spec(a.shape) for a in lw]
    out_shape = (jax.ShapeDtypeStruct((bsz, seq, D_MODEL), F32),
                 jax.ShapeDtypeStruct((bsz, HIST, C_A), F32),
                 jax.ShapeDtypeStruct((bsz, CHUNK, C_B), F32))
    out_specs = (row_spec,
                 pl.BlockSpec((None, HIST, C_A), lambda b, t: (b, 0, 0)),
                 pl.BlockSpec((None, CHUNK, C_B), lambda b, t: (b, 0, 0)))
    return pl.pallas_call(
        _mixer_prompt_kernel,
        grid=grid,
        in_specs=in_specs,
        out_specs=out_specs,
        out_shape=out_shape,
        scratch_shapes=[pltpu.VMEM((C_A // LANES, HIST_PAD + tt, LANES), F32)],
        compiler_params=pltpu.CompilerParams(
            dimension_semantics=("arbitrary", "arbitrary"),
            vmem_limit_bytes=V7X_VMEM_LIMIT_BYTES),
        name="mixer_prompt",
    )(x, *lw)


def _mixer_sample_kernel(x_ref, st_ref, gmix_ref, win_ref, bin_ref, convw_ref, convb_ref,
                         lnag_ref, lnab_ref, waout_ref, lnvg_ref, lnvb_ref, wmix_ref, bmix_ref,
                         wbout_ref, wo_ref, x1_ref, glu_ref, v_ref):
    n_s, nb, _ = x_ref.shape
    x = x_ref[...].reshape(n_s * nb, D_MODEL)
    h = _rms(x, gmix_ref[...]).astype(BF16)

    za = jnp.dot(h, win_ref[:, COL_A:COL_A + 2 * C_A], preferred_element_type=F32)
    za = za + bin_ref[:, COL_A:COL_A + 2 * C_A]
    glu = za[:, :C_A] * jax.nn.sigmoid(za[:, C_A:])
    glu_ref[...] = glu.reshape(n_s, nb, C_A)
    conv_rows = []
    for s in range(n_s):
        acc = jnp.zeros((nb, C_A), F32) + convb_ref[...]
        for j in range(s, HIST):
            acc = acc + convw_ref[j - s:j - s + 1, :] * st_ref[j]
        for r in range(s + 1):
            k = HIST - s + r
            acc = acc + convw_ref[k:k + 1, :] * glu[r * nb:(r + 1) * nb, :]
        conv_rows.append(acc)
    ya = _ln(jnp.concatenate(conv_rows, axis=0), lnag_ref[...], lnab_ref[...])
    ya = _dot(ya * jax.nn.sigmoid(ya), waout_ref[...])

    zb = jnp.dot(h, win_ref[:, COL_B:COL_B + 2 * C_B], preferred_element_type=F32)
    zb = jax.nn.gelu(zb + bin_ref[:, COL_B:COL_B + 2 * C_B])
    u = zb[:, :C_B]
    v = _ln(zb[:, C_B:], lnvg_ref[...], lnvb_ref[...])
    v_ref[...] = v.reshape(n_s, nb, C_B)
    mixed_rows = []
    for s in range(n_s):
        acc = jnp.zeros((nb, C_B), F32) + bmix_ref[s:s + 1, :]
        for r in range(s + 1):
            acc = acc + wmix_ref[s, r:r + 1, :] * v[r * nb:(r + 1) * nb, :]
        mixed_rows.append(acc)
    mixed = jnp.concatenate(mixed_rows, axis=0)
    yb = _dot(u * mixed, wbout_ref[...])

    zg = jnp.dot(h, win_ref[:, COL_G:COL_G + 2 * D_MODEL], preferred_element_type=F32)
    zg = zg + bin_ref[:, COL_G:COL_G + 2 * D_MODEL]
    m = jax.nn.sigmoid(zg[:, :D_MODEL]) * ya + jax.nn.sigmoid(zg[:, D_MODEL:]) * yb
    x1_ref[...] = (x + _dot(m, wo_ref[...])).reshape(n_s, nb, D_MODEL)


SAMPLE_SEQS = 64


def _mixer_sample(x_t, st_t, lw):
    n_s, nseq, _ = x_t.shape
    nb = SAMPLE_SEQS
    grid = (nseq // nb,)
    x_spec = pl.BlockSpec((n_s, nb, D_MODEL), lambda i: (0, i, 0))
    in_specs = [x_spec, pl.BlockSpec((HIST, nb, C_A), lambda i: (0, i, 0))]
    in_specs += [_const_spec(a.shape) for a in lw]
    out_shape = (jax.ShapeDtypeStruct((n_s, nseq, D_MODEL), F32),
                 jax.ShapeDtypeStruct((n_s, nseq, C_A), F32),
                 jax.ShapeDtypeStruct((n_s, nseq, C_B), F32))
    out_specs = (x_spec,
                 pl.BlockSpec((n_s, nb, C_A), lambda i: (0, i, 0)),
                 pl.BlockSpec((n_s, nb, C_B), lambda i: (0, i, 0)))
    return pl.pallas_call(
        _mixer_sample_kernel,
        grid=grid,
        in_specs=in_specs,
        out_specs=out_specs,
        out_shape=out_shape,
        compiler_params=pltpu.CompilerParams(
            dimension_semantics=("arbitrary",),
            vmem_limit_bytes=V7X_VMEM_LIMIT_BYTES),
        name="mixer_sample",
    )(x_t, st_t, *lw)


def _ffn_kernel(x_ref, p_ref, gffn_ref, w1_ref, w2_ref, gple_ref, wple_ref, wpg_ref, bpg_ref,
                gfin_ref, y_ref, *, final_norm):
    x = x_ref[...]
    hf = _rms(x, gffn_ref[...])
    a = jnp.maximum(_dot(hf, w1_ref[...]), 0.0)
    x = x + _dot(a * a, w2_ref[...])
    gate = jax.nn.sigmoid(_dot(_rms(x, gple_ref[...]), wpg_ref[...]) + bpg_ref[...])
    x = x + gate * _dot(p_ref[...], wple_ref[...])
    if final_norm:
        x = _rms(x, gfin_ref[...])
    y_ref[...] = x


def _ffn(x2d, p2d, fw, final_norm):
    rows = x2d.shape[0]
    tr = min(FFN_ROWS, rows)
    grid = (rows // tr,)
    in_specs = [pl.BlockSpec((tr, D_MODEL), lambda i: (i, 0)),
                pl.BlockSpec((tr, P_DIM), lambda i: (i, 0))]
    in_specs += [_const_spec(a.shape) for a in fw]
    return pl.pallas_call(
        functools.partial(_ffn_kernel, final_norm=final_norm),
        grid=grid,
        in_specs=in_specs,
        out_specs=pl.BlockSpec((tr, D_MODEL), lambda i: (i, 0)),
        out_shape=jax.ShapeDtypeStruct((rows, D_MODEL), F32),
        compiler_params=pltpu.CompilerParams(
            dimension_semantics=("arbitrary",),
            vmem_limit_bytes=V7X_VMEM_LIMIT_BYTES),
        name="ffn",
    )(x2d, p2d, *fw)


def _row(a):
    return a.reshape(1, -1)


def kernel(x_prompt, x_sample, state_conv, p_prompt, p_sample, g_mix, w_in, b_in, conv_w, conv_b,
           ln_a_g, ln_a_b, w_a_out, ln_v_g, ln_v_b, w_s, b_s, w_b_out, w_o, g_ffn, w_ff1, w_ff2,
           g_ple, w_ple, w_ple_gate, b_ple_gate, g_final):
    depth = w_in.shape[0]
    bsz, seq, _ = x_prompt.shape
    nseq, n_s, _ = x_sample.shape

    xp = x_prompt
    xs_t = jnp.transpose(x_sample, (1, 0, 2))
    conv_p, conv_s, v_p, v_s = [], [], [], []
    for i in range(depth):
        shared = (_row(g_mix[i]), w_in[i].astype(BF16), _row(b_in[i]), conv_w[i], _row(conv_b[i]),
                  _row(ln_a_g[i]), _row(ln_a_b[i]), w_a_out[i].astype(BF16),
                  _row(ln_v_g[i]), _row(ln_v_b[i]))
        tail = (w_b_out[i].astype(BF16), w_o[i].astype(BF16))
        ws_pairs = jnp.concatenate([w_s[i, 0::2], w_s[i, 1::2]], axis=2).astype(BF16)
        bs_full = jnp.repeat(jnp.transpose(b_s[i]), GROUP_DIM, axis=1)
        wmix = jnp.repeat(jnp.transpose(w_s[i, :, :n_s, :n_s], (1, 2, 0)), GROUP_DIM, axis=2)
        fw = (_row(g_ffn[i]), w_ff1[i].astype(BF16), w_ff2[i].astype(BF16), _row(g_ple[i]),
              w_ple[i].astype(BF16), w_ple_gate[i].astype(BF16), _row(b_ple_gate[i]),
              _row(g_final))
        last = i == depth - 1

        xp, cp, vp = _mixer_prompt(xp, shared + (ws_pairs, bs_full) + tail)
        xp = _ffn(xp.reshape(bsz * seq, D_MODEL), p_prompt[i].reshape(bsz * seq, P_DIM), fw,
                  last).reshape(bsz, seq, D_MODEL)

        st_t = jnp.transpose(state_conv[i], (1, 0, 2))
        xs_t, glu_t, v_t = _mixer_sample(xs_t, st_t, shared + (wmix, bs_full[:n_s]) + tail)
        p_t = jnp.transpose(p_sample[i], (1, 0, 2)).reshape(n_s * nseq, P_DIM)
        xs_t = _ffn(xs_t.reshape(n_s * nseq, D_MODEL), p_t, fw, last).reshape(n_s, nseq, D_MODEL)

        conv_p.append(cp)
        v_p.append(vp)
        conv_s.append(jnp.concatenate([state_conv[i][:, n_s:], jnp.transpose(glu_t, (1, 0, 2))],
                                      axis=1))
        v_s.append(jnp.transpose(v_t, (1, 0, 2)))

    y_sample = jnp.transpose(xs_t, (1, 0, 2))
    return (xp, y_sample, jnp.stack(conv_p), jnp.stack(conv_s), jnp.stack(v_p), jnp.stack(v_s))
```

```python
import functools

import jax
import jax.numpy as jnp
from jax.experimental import pallas as pl
from jax.experimental.pallas import tpu as pltpu

D_MODEL = 1024
C_A = D_MODEL
C_B = (3 * D_MODEL) // 2
CONV_WIDTH = 31
HIST = CONV_WIDTH - 1
CHUNK = 128
N_GROUPS = 8
GROUP_DIM = C_B // N_GROUPS
PAIR_DIM = 2 * GROUP_DIM
N_PAIRS = N_GROUPS // 2
D_FF = 4 * D_MODEL
P_DIM = 256
EPS = 1e-6

COL_A = 0
COL_B = 2 * C_A
COL_G = 2 * C_A + 2 * C_B
D_IN = 2 * C_A + 2 * C_B + 2 * D_MODEL

LANES = 128
HIST_PAD = 32
TILE_T = 256
FFN_ROWS = 512
V7X_VMEM_LIMIT_BYTES = 56 * 1024 * 1024

BF16 = jnp.bfloat16
F32 = jnp.float32


def _rms(x, g):
    return x * jax.lax.rsqrt(jnp.mean(x * x, axis=-1, keepdims=True) + EPS) * g


def _ln(x, g, b):
    mu = jnp.mean(x, axis=-1, keepdims=True)
    xc = x - mu
    return xc * jax.lax.rsqrt(jnp.mean(xc * xc, axis=-1, keepdims=True) + EPS) * g + b


def _dot(a, b):
    return jnp.dot(a.astype(BF16), b, preferred_element_type=F32)


def _const_spec(arr, layer=None):
    if layer is None:
        shape, index = arr.shape, (0,) * arr.ndim
    else:
        shape, index = (None,) + arr.shape[1:], (layer,) + (0,) * (arr.ndim - 1)
    return pl.BlockSpec(shape, lambda *_: index, pipeline_mode=pl.Buffered(1))


def _mixer_prompt_kernel(x_ref, gmix_ref, win_ref, bin_ref, convw_ref, convb_ref, lnag_ref,
                         lnab_ref, waout_ref, lnvg_ref, lnvb_ref, ws_ref, bsp_ref, wbout_ref,
                         wo_ref, x1_ref, conv_ref, vlast_ref, cbuf_ref):
    t = pl.program_id(1)
    n_t = pl.num_programs(1)
    tt = x_ref.shape[0]

    @pl.when(t == 0)
    def _():
        cbuf_ref[:, 0:HIST_PAD, :] = jnp.zeros((C_A // LANES, HIST_PAD, LANES), F32)

    x = x_ref[...]
    h = _rms(x, gmix_ref[...]).astype(BF16)

    za = jnp.dot(h, win_ref[:, COL_A:COL_A + 2 * C_A], preferred_element_type=F32)
    za = za + bin_ref[:, COL_A:COL_A + 2 * C_A]
    glu = za[:, :C_A] * jax.nn.sigmoid(za[:, C_A:])
    strips = []
    for l in range(C_A // LANES):
        lanes = slice(l * LANES, (l + 1) * LANES)
        cbuf_ref[l, HIST_PAD:HIST_PAD + tt, :] = glu[:, lanes]
        acc = jnp.zeros((tt, LANES), F32) + convb_ref[:, lanes]
        for k in range(CONV_WIDTH):
            off = HIST_PAD - HIST + k
            acc = acc + convw_ref[k:k + 1, lanes] * cbuf_ref[l, off:off + tt, :]
        cbuf_ref[l, 0:HIST_PAD, :] = cbuf_ref[l, tt:tt + HIST_PAD, :]
        strips.append(acc)
    ya = _ln(jnp.concatenate(strips, axis=1), lnag_ref[...], lnab_ref[...])
    ya = _dot(ya * jax.nn.sigmoid(ya), waout_ref[...])

    zb = jnp.dot(h, win_ref[:, COL_B:COL_B + 2 * C_B], preferred_element_type=F32)
    zb = jax.nn.gelu(zb + bin_ref[:, COL_B:COL_B + 2 * C_B])
    u = zb[:, :C_B]
    v = _ln(zb[:, C_B:], lnvg_ref[...], lnvb_ref[...])
    vb = v.astype(BF16)
    row = jax.lax.broadcasted_iota(jnp.int32, (CHUNK, 2 * CHUNK), 0)
    col = jax.lax.broadcasted_iota(jnp.int32, (CHUNK, 2 * CHUNK), 1)
    causal = (col % CHUNK) <= row
    first_group = jax.lax.broadcasted_iota(jnp.int32, (CHUNK, PAIR_DIM), 1) < GROUP_DIM
    zero = jnp.zeros((CHUNK, PAIR_DIM), BF16)
    mixed_chunks = []
    for c in range(tt // CHUNK):
        pieces = []
        for p in range(N_PAIRS):
            wp = jnp.where(causal, ws_ref[p], jnp.zeros_like(ws_ref[p]))
            vs = vb[c * CHUNK:(c + 1) * CHUNK, p * PAIR_DIM:(p + 1) * PAIR_DIM]
            rhs = jnp.concatenate([jnp.where(first_group, vs, zero),
                                   jnp.where(first_group, zero, vs)], axis=0)
            pieces.append(jnp.dot(wp, rhs, preferred_element_type=F32))
        mixed_chunks.append(jnp.concatenate(pieces, axis=1) + bsp_ref[...])
    mixed = jnp.concatenate(mixed_chunks, axis=0)
    yb = _dot(u * mixed, wbout_ref[...])

    zg = jnp.dot(h, win_ref[:, COL_G:COL_G + 2 * D_MODEL], preferred_element_type=F32)
    zg = zg + bin_ref[:, COL_G:COL_G + 2 * D_MODEL]
    m = jax.nn.sigmoid(zg[:, :D_MODEL]) * ya + jax.nn.sigmoid(zg[:, D_MODEL:]) * yb
    x1_ref[...] = x + _dot(m, wo_ref[...])

    @pl.when(t == n_t - 1)
    def _():
        conv_ref[...] = glu[tt - HIST:, :]
        vlast_ref[...] = v[tt - CHUNK:, :]


def _mixer_prompt(x, lw, layer):
    bsz, seq, _ = x.shape
    tt = TILE_T
    grid = (bsz, seq // tt)
    row_spec = pl.BlockSpec((None, tt, D_MODEL), lambda b, t: (b, t, 0))
    in_specs = [row_spec] + [_const_spec(a, layer) for a in lw]
    out_shape = (jax.ShapeDtypeStruct((bsz, seq, D_MODEL), F32),
                 jax.ShapeDtypeStruct((bsz, HIST, C_A), F32),
                 jax.ShapeDtypeStruct((bsz, CHUNK, C_B), F32))
    out_specs = (row_spec,
                 pl.BlockSpec((None, HIST, C_A), lambda b, t: (b, 0, 0)),
                 pl.BlockSpec((None, CHUNK, C_B), lambda b, t: (b, 0, 0)))
    return pl.pallas_call(
        _mixer_prompt_kernel,
        grid=grid,
        in_specs=in_specs,
        out_specs=out_specs,
        out_shape=out_shape,
        scratch_shapes=[pltpu.VMEM((C_A // LANES, HIST_PAD + tt, LANES), F32)],
        compiler_params=pltpu.CompilerParams(
            dimension_semantics=("arbitrary", "arbitrary"),
            vmem_limit_bytes=V7X_VMEM_LIMIT_BYTES),
        name="mixer_prompt",
    )(x, *lw)


def _mixer_sample_kernel(x_ref, st_ref, gmix_ref, win_ref, bin_ref, convw_ref, convb_ref,
                         lnag_ref, lnab_ref, waout_ref, lnvg_ref, lnvb_ref, wmix_ref, bmix_ref,
                         wbout_ref, wo_ref, x1_ref, glu_ref, v_ref):
    n_s, nb, _ = x_ref.shape
    x = x_ref[...].reshape(n_s * nb, D_MODEL)
    h = _rms(x, gmix_ref[...]).astype(BF16)

    za = jnp.dot(h, win_ref[:, COL_A:COL_A + 2 * C_A], preferred_element_type=F32)
    za = za + bin_ref[:, COL_A:COL_A + 2 * C_A]
    glu = za[:, :C_A] * jax.nn.sigmoid(za[:, C_A:])
    glu_ref[...] = glu.reshape(n_s, nb, C_A)
    conv_rows = []
    for s in range(n_s):
        acc = jnp.zeros((nb, C_A), F32) + convb_ref[...]
        for j in range(s, HIST):
            acc = acc + convw_ref[j - s:j - s + 1, :] * st_ref[j]
        for r in range(s + 1):
            k = HIST - s + r
            acc = acc + convw_ref[k:k + 1, :] * glu[r * nb:(r + 1) * nb, :]
        conv_rows.append(acc)
    ya = _ln(jnp.concatenate(conv_rows, axis=0), lnag_ref[...], lnab_ref[...])
    ya = _dot(ya * jax.nn.sigmoid(ya), waout_ref[...])

    zb = jnp.dot(h, win_ref[:, COL_B:COL_B + 2 * C_B], preferred_element_type=F32)
    zb = jax.nn.gelu(zb + bin_ref[:, COL_B:COL_B + 2 * C_B])
    u = zb[:, :C_B]
    v = _ln(zb[:, C_B:], lnvg_ref[...], lnvb_ref[...])
    v_ref[...] = v.reshape(n_s, nb, C_B)
    mixed_rows = []
    for s in range(n_s):
        acc = jnp.zeros((nb, C_B), F32) + bmix_ref[s:s + 1, :]
        for r in range(s + 1):
            acc = acc + wmix_ref[s, r:r + 1, :] * v[r * nb:(r + 1) * nb, :]
        mixed_rows.append(acc)
    mixed = jnp.concatenate(mixed_rows, axis=0)
    yb = _dot(u * mixed, wbout_ref[...])

    zg = jnp.dot(h, win_ref[:, COL_G:COL_G + 2 * D_MODEL], preferred_element_type=F32)
    zg = zg + bin_ref[:, COL_G:COL_G + 2 * D_MODEL]
    m = jax.nn.sigmoid(zg[:, :D_MODEL]) * ya + jax.nn.sigmoid(zg[:, D_MODEL:]) * yb
    x1_ref[...] = (x + _dot(m, wo_ref[...])).reshape(n_s, nb, D_MODEL)


SAMPLE_SEQS = 64


def _mixer_sample(x_t, st_t, lw, layer):
    n_s, nseq, _ = x_t.shape
    nb = SAMPLE_SEQS
    grid = (nseq // nb,)
    x_spec = pl.BlockSpec((n_s, nb, D_MODEL), lambda i: (0, i, 0))
    in_specs = [x_spec, pl.BlockSpec((HIST, nb, C_A), lambda i: (0, i, 0))]
    in_specs += [_const_spec(a, layer) for a in lw]
    out_shape = (jax.ShapeDtypeStruct((n_s, nseq, D_MODEL), F32),
                 jax.ShapeDtypeStruct((n_s, nseq, C_A), F32),
                 jax.ShapeDtypeStruct((n_s, nseq, C_B), F32))
    out_specs = (x_spec,
                 pl.BlockSpec((n_s, nb, C_A), lambda i: (0, i, 0)),
                 pl.BlockSpec((n_s, nb, C_B), lambda i: (0, i, 0)))
    return pl.pallas_call(
        _mixer_sample_kernel,
        grid=grid,
        in_specs=in_specs,
        out_specs=out_specs,
        out_shape=out_shape,
        compiler_params=pltpu.CompilerParams(
            dimension_semantics=("arbitrary",),
            vmem_limit_bytes=V7X_VMEM_LIMIT_BYTES),
        name="mixer_sample",
    )(x_t, st_t, *lw)


def _ffn_kernel(x_ref, p_ref, gffn_ref, w1_ref, w2_ref, gple_ref, wple_ref, wpg_ref, bpg_ref,
                gfin_ref, y_ref, *, final_norm):
    x = x_ref[...]
    hf = _rms(x, gffn_ref[...])
    a = jnp.maximum(_dot(hf, w1_ref[...]), 0.0)
    x = x + _dot(a * a, w2_ref[...])
    gate = jax.nn.sigmoid(_dot(_rms(x, gple_ref[...]), wpg_ref[...]) + bpg_ref[...])
    x = x + gate * _dot(p_ref[...], wple_ref[...])
    if final_norm:
        x = _rms(x, gfin_ref[...])
    y_ref[...] = x


def _ffn(x2d, p2d, fw, g_final, layer, final_norm):
    rows = x2d.shape[0]
    tr = min(FFN_ROWS, rows)
    grid = (rows // tr,)
    in_specs = [pl.BlockSpec((tr, D_MODEL), lambda i: (i, 0)),
                pl.BlockSpec((tr, P_DIM), lambda i: (i, 0))]
    in_specs += [_const_spec(a, layer) for a in fw] + [_const_spec(g_final)]
    return pl.pallas_call(
        functools.partial(_ffn_kernel, final_norm=final_norm),
        grid=grid,
        in_specs=in_specs,
        out_specs=pl.BlockSpec((tr, D_MODEL), lambda i: (i, 0)),
        out_shape=jax.ShapeDtypeStruct((rows, D_MODEL), F32),
        compiler_params=pltpu.CompilerParams(
            dimension_semantics=("arbitrary",),
            vmem_limit_bytes=V7X_VMEM_LIMIT_BYTES),
        name="ffn",
    )(x2d, p2d, *fw, g_final)


def _rows(a):
    return a.reshape(a.shape[0], 1, a.shape[1])


def kernel(x_prompt, x_sample, state_conv, p_prompt, p_sample, g_mix, w_in, b_in, conv_w, conv_b,
           ln_a_g, ln_a_b, w_a_out, ln_v_g, ln_v_b, w_s, b_s, w_b_out, w_o, g_ffn, w_ff1, w_ff2,
           g_ple, w_ple, w_ple_gate, b_ple_gate, g_final):
    depth = w_in.shape[0]
    bsz, seq, _ = x_prompt.shape
    nseq, n_s, _ = x_sample.shape

    shared = (_rows(g_mix), w_in.astype(BF16), _rows(b_in), conv_w, _rows(conv_b),
              _rows(ln_a_g), _rows(ln_a_b), w_a_out.astype(BF16), _rows(ln_v_g), _rows(ln_v_b))
    tail = (w_b_out.astype(BF16), w_o.astype(BF16))
    ws_pairs = jnp.concatenate([w_s[:, 0::2], w_s[:, 1::2]], axis=3).astype(BF16)
    bs_full = jnp.repeat(jnp.transpose(b_s, (0, 2, 1)), GROUP_DIM, axis=2)
    wmix = jnp.repeat(jnp.transpose(w_s[:, :, :n_s, :n_s], (0, 2, 3, 1)), GROUP_DIM, axis=3)
    fw = (_rows(g_ffn), w_ff1.astype(BF16), w_ff2.astype(BF16), _rows(g_ple),
          w_ple.astype(BF16), w_ple_gate.astype(BF16), _rows(b_ple_gate))
    g_fin = g_final.reshape(1, D_MODEL)

    xp = x_prompt
    xs_t = jnp.transpose(x_sample, (1, 0, 2))
    conv_p, conv_s, v_p, v_s = [], [], [], []
    for i in range(depth):
        last = i == depth - 1

        xp, cp, vp = _mixer_prompt(xp, shared + (ws_pairs, bs_full) + tail, i)
        xp = _ffn(xp.reshape(bsz * seq, D_MODEL), p_prompt[i].reshape(bsz * seq, P_DIM), fw,
                  g_fin, i, last).reshape(bsz, seq, D_MODEL)

        st_t = jnp.transpose(state_conv[i], (1, 0, 2))
        xs_t, glu_t, v_t = _mixer_sample(xs_t, st_t, shared + (wmix, bs_full[:, :n_s]) + tail, i)
        p_t = jnp.transpose(p_sample[i], (1, 0, 2)).reshape(n_s * nseq, P_DIM)
        xs_t = _ffn(xs_t.reshape(n_s * nseq, D_MODEL), p_t, fw, g_fin, i,
                    last).reshape(n_s, nseq, D_MODEL)

        conv_p.append(cp)
        v_p.append(vp)
        conv_s.append(jnp.concatenate([state_conv[i][:, n_s:], jnp.transpose(glu_t, (1, 0, 2))],
                                      axis=1))
        v_s.append(jnp.transpose(v_t, (1, 0, 2)))

    y_sample = jnp.transpose(xs_t, (1, 0, 2))
    return (xp, y_sample, jnp.stack(conv_p), jnp.stack(conv_s), jnp.stack(v_p), jnp.stack(v_s))
```

```python
import functools

import jax
import jax.numpy as jnp
from jax.experimental import pallas as pl
from jax.experimental.pallas import tpu as pltpu

D_MODEL = 1024
C_A = D_MODEL
C_B = (3 * D_MODEL) // 2
CONV_WIDTH = 31
HIST = CONV_WIDTH - 1
CHUNK = 128
N_GROUPS = 8
GROUP_DIM = C_B // N_GROUPS
PAIR_DIM = 2 * GROUP_DIM
N_PAIRS = N_GROUPS // 2
D_FF = 4 * D_MODEL
P_DIM = 256
EPS = 1e-6

COL_A = 0
COL_B = 2 * C_A
COL_G = 2 * C_A + 2 * C_B
D_IN = 2 * C_A + 2 * C_B + 2 * D_MODEL

LANES = 128
HIST_PAD = 32
TILE_T = 512
FFN_ROWS = 512
V7X_VMEM_LIMIT_BYTES = 56 * 1024 * 1024

BF16 = jnp.bfloat16
F32 = jnp.float32


def _rms(x, g):
    return x * jax.lax.rsqrt(jnp.mean(x * x, axis=-1, keepdims=True) + EPS) * g


def _ln(x, g, b):
    mu = jnp.mean(x, axis=-1, keepdims=True)
    xc = x - mu
    return xc * jax.lax.rsqrt(jnp.mean(xc * xc, axis=-1, keepdims=True) + EPS) * g + b


def _dot(a, b):
    return jnp.dot(a.astype(BF16), b, preferred_element_type=F32)


def _const_spec(arr, layer=None):
    if layer is None:
        shape, index = arr.shape, (0,) * arr.ndim
    else:
        shape, index = (None,) + arr.shape[1:], (layer,) + (0,) * (arr.ndim - 1)
    return pl.BlockSpec(shape, lambda *_: index, pipeline_mode=pl.Buffered(1))


def _cast_stream(stacked, layer, n_steps, step_of):
    _, rows, cols = stacked.shape
    slab = rows // n_steps
    assert slab * n_steps == rows and slab % 16 == 0, (stacked.shape, n_steps)
    in_spec = pl.BlockSpec((None, slab, cols), lambda *g: (layer, step_of(*g), 0))
    out_spec = pl.BlockSpec((slab, cols), lambda *g: (step_of(*g), 0))
    return in_spec, out_spec, jax.ShapeDtypeStruct((rows, cols), BF16)


def _cast_slabs(refs):
    n = len(refs) // 2
    for src, dst in zip(refs[:n], refs[n:]):
        dst[...] = src[...].astype(BF16)


N_MIXER_CONSTS = 14


def _mixer_prompt_kernel(*refs, n_cast):
    x_ref = refs[0]
    (gmix_ref, win_ref, bin_ref, convw_ref, convb_ref, lnag_ref, lnab_ref, waout_ref, lnvg_ref,
     lnvb_ref, ws_ref, bsp_ref, wbout_ref, wo_ref) = refs[1:1 + N_MIXER_CONSTS]
    n_in = 1 + N_MIXER_CONSTS + n_cast
    x1_ref, conv_ref, vlast_ref = refs[n_in:n_in + 3]
    cbuf_ref = refs[-1]
    _cast_slabs(refs[1 + N_MIXER_CONSTS:n_in] + refs[n_in + 3:-1])
    t = pl.program_id(1)
    n_t = pl.num_programs(1)
    tt = x_ref.shape[0]

    @pl.when(t == 0)
    def _():
        cbuf_ref[:, 0:HIST_PAD, :] = jnp.zeros((C_A // LANES, HIST_PAD, LANES), F32)

    x = x_ref[...]
    h = _rms(x, gmix_ref[...]).astype(BF16)

    za = jnp.dot(h, win_ref[:, COL_A:COL_A + 2 * C_A], preferred_element_type=F32)
    za = za + bin_ref[:, COL_A:COL_A + 2 * C_A]
    glu = za[:, :C_A] * jax.nn.sigmoid(za[:, C_A:])
    strips = []
    for l in range(C_A // LANES):
        lanes = slice(l * LANES, (l + 1) * LANES)
        cbuf_ref[l, HIST_PAD:HIST_PAD + tt, :] = glu[:, lanes]
        acc = jnp.zeros((tt, LANES), F32) + convb_ref[:, lanes]
        for k in range(CONV_WIDTH):
            off = HIST_PAD - HIST + k
            acc = acc + convw_ref[k:k + 1, lanes] * cbuf_ref[l, off:off + tt, :]
        cbuf_ref[l, 0:HIST_PAD, :] = cbuf_ref[l, tt:tt + HIST_PAD, :]
        strips.append(acc)
    ya = _ln(jnp.concatenate(strips, axis=1), lnag_ref[...], lnab_ref[...])
    ya = _dot(ya * jax.nn.sigmoid(ya), waout_ref[...])

    zb = jnp.dot(h, win_ref[:, COL_B:COL_B + 2 * C_B], preferred_element_type=F32)
    zb = jax.nn.gelu(zb + bin_ref[:, COL_B:COL_B + 2 * C_B])
    u = zb[:, :C_B]
    v = _ln(zb[:, C_B:], lnvg_ref[...], lnvb_ref[...])
    vb = v.astype(BF16)
    row = jax.lax.broadcasted_iota(jnp.int32, (CHUNK, 2 * CHUNK), 0)
    col = jax.lax.broadcasted_iota(jnp.int32, (CHUNK, 2 * CHUNK), 1)
    causal = (col % CHUNK) <= row
    first_group = jax.lax.broadcasted_iota(jnp.int32, (CHUNK, PAIR_DIM), 1) < GROUP_DIM
    zero = jnp.zeros((CHUNK, PAIR_DIM), BF16)
    mixed_chunks = []
    for c in range(tt // CHUNK):
        pieces = []
        for p in range(N_PAIRS):
            wp = jnp.where(causal, ws_ref[p], jnp.zeros_like(ws_ref[p]))
            vs = vb[c * CHUNK:(c + 1) * CHUNK, p * PAIR_DIM:(p + 1) * PAIR_DIM]
            rhs = jnp.concatenate([jnp.where(first_group, vs, zero),
                                   jnp.where(first_group, zero, vs)], axis=0)
            pieces.append(jnp.dot(wp, rhs, preferred_element_type=F32))
        mixed_chunks.append(jnp.concatenate(pieces, axis=1) + bsp_ref[...])
    mixed = jnp.concatenate(mixed_chunks, axis=0)
    yb = _dot(u * mixed, wbout_ref[...])

    zg = jnp.dot(h, win_ref[:, COL_G:COL_G + 2 * D_MODEL], preferred_element_type=F32)
    zg = zg + bin_ref[:, COL_G:COL_G + 2 * D_MODEL]
    m = jax.nn.sigmoid(zg[:, :D_MODEL]) * ya + jax.nn.sigmoid(zg[:, D_MODEL:]) * yb
    x1_ref[...] = x + _dot(m, wo_ref[...])

    @pl.when(t == n_t - 1)
    def _():
        conv_ref[...] = glu[tt - HIST:, :]
        vlast_ref[...] = v[tt - CHUNK:, :]


def _mixer_prompt(x, lw, layer, cast_src):
    bsz, seq, _ = x.shape
    tt = TILE_T
    n_t = seq // tt
    grid = (bsz, n_t)
    row_spec = pl.BlockSpec((None, tt, D_MODEL), lambda b, t: (b, t, 0))
    streams = [_cast_stream(w, layer, bsz * n_t, lambda b, t: b * n_t + t) for w in cast_src]
    in_specs = [row_spec] + [_const_spec(a, l) for a, l in lw] + [s[0] for s in streams]
    out_shape = (jax.ShapeDtypeStruct((bsz, seq, D_MODEL), F32),
                 jax.ShapeDtypeStruct((bsz, HIST, C_A), F32),
                 jax.ShapeDtypeStruct((bsz, CHUNK, C_B), F32)) + tuple(s[2] for s in streams)
    out_specs = (row_spec,
                 pl.BlockSpec((None, HIST, C_A), lambda b, t: (b, 0, 0)),
                 pl.BlockSpec((None, CHUNK, C_B), lambda b, t: (b, 0, 0))
                 ) + tuple(s[1] for s in streams)
    return pl.pallas_call(
        functools.partial(_mixer_prompt_kernel, n_cast=len(cast_src)),
        grid=grid,
        in_specs=in_specs,
        out_specs=out_specs,
        out_shape=out_shape,
        scratch_shapes=[pltpu.VMEM((C_A // LANES, HIST_PAD + tt, LANES), F32)],
        compiler_params=pltpu.CompilerParams(
            dimension_semantics=("arbitrary", "arbitrary"),
            vmem_limit_bytes=V7X_VMEM_LIMIT_BYTES),
        name="mixer_prompt",
    )(x, *[a for a, _ in lw], *cast_src)


def _mixer_sample_kernel(x_ref, st_ref, gmix_ref, win_ref, bin_ref, convw_ref, convb_ref,
                         lnag_ref, lnab_ref, waout_ref, lnvg_ref, lnvb_ref, wmix_ref, bmix_ref,
                         wbout_ref, wo_ref, x1_ref, glu_ref, v_ref):
    n_s, nb, _ = x_ref.shape
    x = x_ref[...].reshape(n_s * nb, D_MODEL)
    h = _rms(x, gmix_ref[...]).astype(BF16)

    za = jnp.dot(h, win_ref[:, COL_A:COL_A + 2 * C_A], preferred_element_type=F32)
    za = za + bin_ref[:, COL_A:COL_A + 2 * C_A]
    glu = za[:, :C_A] * jax.nn.sigmoid(za[:, C_A:])
    glu_ref[...] = glu.reshape(n_s, nb, C_A)
    conv_rows = []
    for s in range(n_s):
        acc = jnp.zeros((nb, C_A), F32) + convb_ref[...]
        for j in range(s, HIST):
            acc = acc + convw_ref[j - s:j - s + 1, :] * st_ref[j]
        for r in range(s + 1):
            k = HIST - s + r
            acc = acc + convw_ref[k:k + 1, :] * glu[r * nb:(r + 1) * nb, :]
        conv_rows.append(acc)
    ya = _ln(jnp.concatenate(conv_rows, axis=0), lnag_ref[...], lnab_ref[...])
    ya = _dot(ya * jax.nn.sigmoid(ya), waout_ref[...])

    zb = jnp.dot(h, win_ref[:, COL_B:COL_B + 2 * C_B], preferred_element_type=F32)
    zb = jax.nn.gelu(zb + bin_ref[:, COL_B:COL_B + 2 * C_B])
    u = zb[:, :C_B]
    v = _ln(zb[:, C_B:], lnvg_ref[...], lnvb_ref[...])
    v_ref[...] = v.reshape(n_s, nb, C_B)
    mixed_rows = []
    for s in range(n_s):
        acc = jnp.zeros((nb, C_B), F32) + bmix_ref[s:s + 1, :]
        for r in range(s + 1):
            acc = acc + wmix_ref[s, r:r + 1, :] * v[r * nb:(r + 1) * nb, :]
        mixed_rows.append(acc)
    mixed = jnp.concatenate(mixed_rows, axis=0)
    yb = _dot(u * mixed, wbout_ref[...])

    zg = jnp.dot(h, win_ref[:, COL_G:COL_G + 2 * D_MODEL], preferred_element_type=F32)
    zg = zg + bin_ref[:, COL_G:COL_G + 2 * D_MODEL]
    m = jax.nn.sigmoid(zg[:, :D_MODEL]) * ya + jax.nn.sigmoid(zg[:, D_MODEL:]) * yb
    x1_ref[...] = (x + _dot(m, wo_ref[...])).reshape(n_s, nb, D_MODEL)


SAMPLE_SEQS = 64


def _mixer_sample(x_t, st_t, lw):
    n_s, nseq, _ = x_t.shape
    nb = SAMPLE_SEQS
    grid = (nseq // nb,)
    x_spec = pl.BlockSpec((n_s, nb, D_MODEL), lambda i: (0, i, 0))
    in_specs = [x_spec, pl.BlockSpec((HIST, nb, C_A), lambda i: (0, i, 0))]
    in_specs += [_const_spec(a, l) for a, l in lw]
    out_shape = (jax.ShapeDtypeStruct((n_s, nseq, D_MODEL), F32),
                 jax.ShapeDtypeStruct((n_s, nseq, C_A), F32),
                 jax.ShapeDtypeStruct((n_s, nseq, C_B), F32))
    out_specs = (x_spec,
                 pl.BlockSpec((n_s, nb, C_A), lambda i: (0, i, 0)),
                 pl.BlockSpec((n_s, nb, C_B), lambda i: (0, i, 0)))
    return pl.pallas_call(
        _mixer_sample_kernel,
        grid=grid,
        in_specs=in_specs,
        out_specs=out_specs,
        out_shape=out_shape,
        compiler_params=pltpu.CompilerParams(
            dimension_semantics=("arbitrary",),
            vmem_limit_bytes=V7X_VMEM_LIMIT_BYTES),
        name="mixer_sample",
    )(x_t, st_t, *[a for a, _ in lw])


N_FFN_CONSTS = 8


def _ffn_kernel(*refs, n_cast, final_norm):
    x_ref, p_ref = refs[:2]
    (gffn_ref, w1_ref, w2_ref, gple_ref, wple_ref, wpg_ref, bpg_ref,
     gfin_ref) = refs[2:2 + N_FFN_CONSTS]
    n_in = 2 + N_FFN_CONSTS + n_cast
    y_ref = refs[n_in]
    _cast_slabs(refs[2 + N_FFN_CONSTS:n_in] + refs[n_in + 1:])
    x = x_ref[...]
    hf = _rms(x, gffn_ref[...])
    a = jnp.maximum(_dot(hf, w1_ref[...]), 0.0)
    x = x + _dot(a * a, w2_ref[...])
    gate = jax.nn.sigmoid(_dot(_rms(x, gple_ref[...]), wpg_ref[...]) + bpg_ref[...])
    x = x + gate * _dot(p_ref[...], wple_ref[...])
    if final_norm:
        x = _rms(x, gfin_ref[...])
    y_ref[...] = x


def _ffn(x2d, p2d, fw, final_norm, cast_src=(), cast_layer=None):
    rows = x2d.shape[0]
    tr = min(FFN_ROWS, rows)
    n_steps = rows // tr
    streams = [_cast_stream(w, cast_layer, n_steps, lambda i: i) for w in cast_src]
    in_specs = [pl.BlockSpec((tr, D_MODEL), lambda i: (i, 0)),
                pl.BlockSpec((tr, P_DIM), lambda i: (i, 0))]
    in_specs += [_const_spec(a, l) for a, l in fw] + [s[0] for s in streams]
    return pl.pallas_call(
        functools.partial(_ffn_kernel, n_cast=len(cast_src), final_norm=final_norm),
        grid=(n_steps,),
        in_specs=in_specs,
        out_specs=(pl.BlockSpec((tr, D_MODEL), lambda i: (i, 0)),) + tuple(s[1] for s in streams),
        out_shape=(jax.ShapeDtypeStruct((rows, D_MODEL), F32),) + tuple(s[2] for s in streams),
        compiler_params=pltpu.CompilerParams(
            dimension_semantics=("arbitrary",),
            vmem_limit_bytes=V7X_VMEM_LIMIT_BYTES),
        name="ffn",
    )(x2d, p2d, *[a for a, _ in fw], *cast_src)


def _rows(a):
    return a.reshape(a.shape[0], 1, a.shape[1])


def kernel(x_prompt, x_sample, state_conv, p_prompt, p_sample, g_mix, w_in, b_in, conv_w, conv_b,
           ln_a_g, ln_a_b, w_a_out, ln_v_g, ln_v_b, w_s, b_s, w_b_out, w_o, g_ffn, w_ff1, w_ff2,
           g_ple, w_ple, w_ple_gate, b_ple_gate, g_final):
    depth = w_in.shape[0]
    bsz, seq, _ = x_prompt.shape
    nseq, n_s, _ = x_sample.shape

    ws_pairs = jnp.concatenate([w_s[:, 0::2], w_s[:, 1::2]], axis=3).astype(BF16)
    bs_full = jnp.repeat(jnp.transpose(b_s, (0, 2, 1)), GROUP_DIM, axis=2)
    wmix = jnp.repeat(jnp.transpose(w_s[:, :, :n_s, :n_s], (0, 2, 3, 1)), GROUP_DIM, axis=3)
    w_ple_bf = w_ple.astype(BF16)
    g_fin = g_final.reshape(1, D_MODEL)
    mixer_src = (w_in, w_a_out, w_b_out, w_o)
    ffn_src = (w_ff1, w_ff2, w_ple_gate)
    mixer_bf = tuple(w[0].astype(BF16) for w in mixer_src)

    xp = x_prompt
    xs_t = jnp.transpose(x_sample, (1, 0, 2))
    conv_p, conv_s, v_p, v_s = [], [], [], []
    for i in range(depth):
        last = i == depth - 1
        win_bf, waout_bf, wbout_bf, wo_bf = mixer_bf
        shared = ((_rows(g_mix), i), (win_bf, None), (_rows(b_in), i), (conv_w, i),
                  (_rows(conv_b), i), (_rows(ln_a_g), i), (_rows(ln_a_b), i), (waout_bf, None),
                  (_rows(ln_v_g), i), (_rows(ln_v_b), i))
        tail = ((wbout_bf, None), (wo_bf, None))

        xp, cp, vp, w1_bf, w2_bf, wpg_bf = _mixer_prompt(
            xp, shared + ((ws_pairs, i), (bs_full, i)) + tail, i, ffn_src)
        fw = ((_rows(g_ffn), i), (w1_bf, None), (w2_bf, None), (_rows(g_ple), i),
              (w_ple_bf, i), (wpg_bf, None), (_rows(b_ple_gate), i), (g_fin, None))
        res = _ffn(xp.reshape(bsz * seq, D_MODEL), p_prompt[i].reshape(bsz * seq, P_DIM), fw, last,
                   () if last else mixer_src, None if last else i + 1)
        xp = res[0].reshape(bsz, seq, D_MODEL)

        st_t = jnp.transpose(state_conv[i], (1, 0, 2))
        xs_t, glu_t, v_t = _mixer_sample(
            xs_t, st_t, shared + ((wmix, i), (bs_full[:, :n_s], i)) + tail)
        p_t = jnp.transpose(p_sample[i], (1, 0, 2)).reshape(n_s * nseq, P_DIM)
        xs_t = _ffn(xs_t.reshape(n_s * nseq, D_MODEL), p_t, fw, last)[0].reshape(
            n_s, nseq, D_MODEL)
        mixer_bf = res[1:]

        conv_p.append(cp)
        v_p.append(vp)
        conv_s.append(jnp.concatenate([state_conv[i][:, n_s:], jnp.transpose(glu_t, (1, 0, 2))],
                                      axis=1))
        v_s.append(jnp.transpose(v_t, (1, 0, 2)))

    y_sample = jnp.transpose(xs_t, (1, 0, 2))
    return (xp, y_sample, jnp.stack(conv_p), jnp.stack(conv_s), jnp.stack(v_p), jnp.stack(v_s))
```

```python
import functools

import jax
import jax.numpy as jnp
from jax.experimental import pallas as pl
from jax.experimental.pallas import tpu as pltpu

D_MODEL = 1024
C_A = D_MODEL
C_B = (3 * D_MODEL) // 2
CONV_WIDTH = 31
HIST = CONV_WIDTH - 1
CHUNK = 128
N_GROUPS = 8
GROUP_DIM = C_B // N_GROUPS
PAIR_DIM = 2 * GROUP_DIM
N_PAIRS = N_GROUPS // 2
D_FF = 4 * D_MODEL
P_DIM = 256
EPS = 1e-6

COL_A = 0
COL_B = 2 * C_A
COL_G = 2 * C_A + 2 * C_B
D_IN = 2 * C_A + 2 * C_B + 2 * D_MODEL

LANES = 128
HIST_PAD = 32
TILE_T = 512
FFN_ROWS = 512
V7X_VMEM_LIMIT_BYTES = 56 * 1024 * 1024

BF16 = jnp.bfloat16
F32 = jnp.float32


def _rms(x, g):
    return x * jax.lax.rsqrt(jnp.mean(x * x, axis=-1, keepdims=True) + EPS) * g


def _ln(x, g, b):
    mu = jnp.mean(x, axis=-1, keepdims=True)
    xc = x - mu
    return xc * jax.lax.rsqrt(jnp.mean(xc * xc, axis=-1, keepdims=True) + EPS) * g + b


def _dot(a, b):
    return jnp.dot(a.astype(BF16), b, preferred_element_type=F32)


def _const_spec(arr, layer=None):
    if layer is None:
        shape, index = arr.shape, (0,) * arr.ndim
    else:
        shape, index = (None,) + arr.shape[1:], (layer,) + (0,) * (arr.ndim - 1)
    return pl.BlockSpec(shape, lambda *_: index, pipeline_mode=pl.Buffered(1))


def _cast_stream(stacked, layer, n_steps, step_of):
    _, rows, cols = stacked.shape
    slab = rows // n_steps
    assert slab * n_steps == rows and slab % 16 == 0, (stacked.shape, n_steps)
    in_spec = pl.BlockSpec((None, slab, cols), lambda *g: (layer, step_of(*g), 0))
    out_spec = pl.BlockSpec((slab, cols), lambda *g: (step_of(*g), 0))
    return in_spec, out_spec, jax.ShapeDtypeStruct((rows, cols), BF16)


def _cast_slabs(refs):
    n = len(refs) // 2
    for src, dst in zip(refs[:n], refs[n:]):
        dst[...] = src[...].astype(BF16)


N_MIXER_CONSTS = 14


def _mixer_prompt_kernel(*refs, n_cast):
    x_ref = refs[0]
    (gmix_ref, win_ref, bin_ref, convw_ref, convb_ref, lnag_ref, lnab_ref, waout_ref, lnvg_ref,
     lnvb_ref, ws_ref, bsp_ref, wbout_ref, wo_ref) = refs[1:1 + N_MIXER_CONSTS]
    n_in = 1 + N_MIXER_CONSTS + n_cast
    x1_ref, conv_ref, vlast_ref = refs[n_in:n_in + 3]
    cbuf_ref = refs[-1]
    _cast_slabs(refs[1 + N_MIXER_CONSTS:n_in] + refs[n_in + 3:-1])
    t = pl.program_id(1)
    n_t = pl.num_programs(1)
    tt = x_ref.shape[0]

    @pl.when(t == 0)
    def _():
        cbuf_ref[:, 0:HIST_PAD, :] = jnp.zeros((C_A // LANES, HIST_PAD, LANES), F32)

    x = x_ref[...]
    h = _rms(x, gmix_ref[...]).astype(BF16)

    za = jnp.dot(h, win_ref[:, COL_A:COL_A + 2 * C_A], preferred_element_type=F32)
    za = za + bin_ref[:, COL_A:COL_A + 2 * C_A]
    glu = za[:, :C_A] * jax.nn.sigmoid(za[:, C_A:])
    strips = []
    for l in range(C_A // LANES):
        lanes = slice(l * LANES, (l + 1) * LANES)
        cbuf_ref[l, HIST_PAD:HIST_PAD + tt, :] = glu[:, lanes]
        acc = jnp.zeros((tt, LANES), F32) + convb_ref[:, lanes]
        for k in range(CONV_WIDTH):
            off = HIST_PAD - HIST + k
            acc = acc + convw_ref[k:k + 1, lanes] * cbuf_ref[l, off:off + tt, :]
        cbuf_ref[l, 0:HIST_PAD, :] = cbuf_ref[l, tt:tt + HIST_PAD, :]
        strips.append(acc)
    ya = _ln(jnp.concatenate(strips, axis=1), lnag_ref[...], lnab_ref[...])
    ya = _dot(ya * jax.nn.sigmoid(ya), waout_ref[...])

    zb = jnp.dot(h, win_ref[:, COL_B:COL_B + 2 * C_B], preferred_element_type=F32)
    zb = jax.nn.gelu(zb + bin_ref[:, COL_B:COL_B + 2 * C_B])
    u = zb[:, :C_B]
    v = _ln(zb[:, C_B:], lnvg_ref[...], lnvb_ref[...])
    vb = v.astype(BF16)
    row = jax.lax.broadcasted_iota(jnp.int32, (CHUNK, 2 * CHUNK), 0)
    col = jax.lax.broadcasted_iota(jnp.int32, (CHUNK, 2 * CHUNK), 1)
    causal = (col % CHUNK) <= row
    first_group = jax.lax.broadcasted_iota(jnp.int32, (CHUNK, PAIR_DIM), 1) < GROUP_DIM
    zero = jnp.zeros((CHUNK, PAIR_DIM), BF16)
    mixed_chunks = []
    for c in range(tt // CHUNK):
        pieces = []
        for p in range(N_PAIRS):
            wp = jnp.where(causal, ws_ref[p], jnp.zeros_like(ws_ref[p]))
            vs = vb[c * CHUNK:(c + 1) * CHUNK, p * PAIR_DIM:(p + 1) * PAIR_DIM]
            rhs = jnp.concatenate([jnp.where(first_group, vs, zero),
                                   jnp.where(first_group, zero, vs)], axis=0)
            pieces.append(jnp.dot(wp, rhs, preferred_element_type=F32))
        mixed_chunks.append(jnp.concatenate(pieces, axis=1) + bsp_ref[...])
    mixed = jnp.concatenate(mixed_chunks, axis=0)
    yb = _dot(u * mixed, wbout_ref[...])

    zg = jnp.dot(h, win_ref[:, COL_G:COL_G + 2 * D_MODEL], preferred_element_type=F32)
    zg = zg + bin_ref[:, COL_G:COL_G + 2 * D_MODEL]
    m = jax.nn.sigmoid(zg[:, :D_MODEL]) * ya + jax.nn.sigmoid(zg[:, D_MODEL:]) * yb
    x1_ref[...] = x + _dot(m, wo_ref[...])

    @pl.when(t == n_t - 1)
    def _():
        conv_ref[...] = glu[tt - HIST:, :]
        vlast_ref[...] = v[tt - CHUNK:, :]


def _mixer_prompt(x, lw, layer, cast_src):
    bsz, seq, _ = x.shape
    tt = TILE_T
    n_t = seq // tt
    grid = (bsz, n_t)
    row_spec = pl.BlockSpec((None, tt, D_MODEL), lambda b, t: (b, t, 0))
    streams = [_cast_stream(w, layer, bsz * n_t, lambda b, t: b * n_t + t) for w in cast_src]
    in_specs = [row_spec] + [_const_spec(a, l) for a, l in lw] + [s[0] for s in streams]
    out_shape = (jax.ShapeDtypeStruct((bsz, seq, D_MODEL), F32),
                 jax.ShapeDtypeStruct((bsz, HIST, C_A), F32),
                 jax.ShapeDtypeStruct((bsz, CHUNK, C_B), F32)) + tuple(s[2] for s in streams)
    out_specs = (row_spec,
                 pl.BlockSpec((None, HIST, C_A), lambda b, t: (b, 0, 0)),
                 pl.BlockSpec((None, CHUNK, C_B), lambda b, t: (b, 0, 0))
                 ) + tuple(s[1] for s in streams)
    return pl.pallas_call(
        functools.partial(_mixer_prompt_kernel, n_cast=len(cast_src)),
        grid=grid,
        in_specs=in_specs,
        out_specs=out_specs,
        out_shape=out_shape,
        scratch_shapes=[pltpu.VMEM((C_A // LANES, HIST_PAD + tt, LANES), F32)],
        compiler_params=pltpu.CompilerParams(
            dimension_semantics=("arbitrary", "arbitrary"),
            vmem_limit_bytes=V7X_VMEM_LIMIT_BYTES),
        name="mixer_prompt",
    )(x, *[a for a, _ in lw], *cast_src)


def _mixer_sample_kernel(x_ref, st_ref, gmix_ref, win_ref, bin_ref, convw_ref, convb_ref,
                         lnag_ref, lnab_ref, waout_ref, lnvg_ref, lnvb_ref, wmix_ref, bmix_ref,
                         wbout_ref, wo_ref, x1_ref, glu_ref, v_ref):
    n_s, nb, _ = x_ref.shape
    x = x_ref[...].reshape(n_s * nb, D_MODEL)
    h = _rms(x, gmix_ref[...]).astype(BF16)

    za = jnp.dot(h, win_ref[:, COL_A:COL_A + 2 * C_A], preferred_element_type=F32)
    za = za + bin_ref[:, COL_A:COL_A + 2 * C_A]
    glu = za[:, :C_A] * jax.nn.sigmoid(za[:, C_A:])
    glu_ref[...] = glu.reshape(n_s, nb, C_A)
    conv_rows = []
    for s in range(n_s):
        acc = jnp.zeros((nb, C_A), F32) + convb_ref[...]
        for j in range(s, HIST):
            acc = acc + convw_ref[j - s:j - s + 1, :] * st_ref[j]
        for r in range(s + 1):
            k = HIST - s + r
            acc = acc + convw_ref[k:k + 1, :] * glu[r * nb:(r + 1) * nb, :]
        conv_rows.append(acc)
    ya = _ln(jnp.concatenate(conv_rows, axis=0), lnag_ref[...], lnab_ref[...])
    ya = _dot(ya * jax.nn.sigmoid(ya), waout_ref[...])

    zb = jnp.dot(h, win_ref[:, COL_B:COL_B + 2 * C_B], preferred_element_type=F32)
    zb = jax.nn.gelu(zb + bin_ref[:, COL_B:COL_B + 2 * C_B])
    u = zb[:, :C_B]
    v = _ln(zb[:, C_B:], lnvg_ref[...], lnvb_ref[...])
    v_ref[...] = v.reshape(n_s, nb, C_B)
    mixed_rows = []
    for s in range(n_s):
        acc = jnp.zeros((nb, C_B), F32) + bmix_ref[s:s + 1, :]
        for r in range(s + 1):
            acc = acc + wmix_ref[s, r:r + 1, :] * v[r * nb:(r + 1) * nb, :]
        mixed_rows.append(acc)
    mixed = jnp.concatenate(mixed_rows, axis=0)
    yb = _dot(u * mixed, wbout_ref[...])

    zg = jnp.dot(h, win_ref[:, COL_G:COL_G + 2 * D_MODEL], preferred_element_type=F32)
    zg = zg + bin_ref[:, COL_G:COL_G + 2 * D_MODEL]
    m = jax.nn.sigmoid(zg[:, :D_MODEL]) * ya + jax.nn.sigmoid(zg[:, D_MODEL:]) * yb
    x1_ref[...] = (x + _dot(m, wo_ref[...])).reshape(n_s, nb, D_MODEL)


SAMPLE_SEQS = 64


def _mixer_sample(x_t, st_t, layer, lw):
    n_s, nseq, _ = x_t.shape
    nb = SAMPLE_SEQS
    grid = (nseq // nb,)
    x_spec = pl.BlockSpec((n_s, nb, D_MODEL), lambda i: (0, i, 0))
    in_specs = [x_spec, pl.BlockSpec((None, HIST, nb, C_A), lambda i: (layer, 0, i, 0))]
    in_specs += [_const_spec(a, l) for a, l in lw]
    out_shape = (jax.ShapeDtypeStruct((n_s, nseq, D_MODEL), F32),
                 jax.ShapeDtypeStruct((n_s, nseq, C_A), F32),
                 jax.ShapeDtypeStruct((n_s, nseq, C_B), F32))
    out_specs = (x_spec,
                 pl.BlockSpec((n_s, nb, C_A), lambda i: (0, i, 0)),
                 pl.BlockSpec((n_s, nb, C_B), lambda i: (0, i, 0)))
    return pl.pallas_call(
        _mixer_sample_kernel,
        grid=grid,
        in_specs=in_specs,
        out_specs=out_specs,
        out_shape=out_shape,
        compiler_params=pltpu.CompilerParams(
            dimension_semantics=("arbitrary",),
            vmem_limit_bytes=V7X_VMEM_LIMIT_BYTES),
        name="mixer_sample",
    )(x_t, st_t, *[a for a, _ in lw])


N_FFN_CONSTS = 8


def _ffn_kernel(*refs, n_cast, final_norm):
    x_ref, p_ref = refs[:2]
    (gffn_ref, w1_ref, w2_ref, gple_ref, wple_ref, wpg_ref, bpg_ref,
     gfin_ref) = refs[2:2 + N_FFN_CONSTS]
    n_in = 2 + N_FFN_CONSTS + n_cast
    y_ref = refs[n_in]
    _cast_slabs(refs[2 + N_FFN_CONSTS:n_in] + refs[n_in + 1:])
    x = x_ref[...]
    hf = _rms(x, gffn_ref[...])
    a = jnp.maximum(_dot(hf, w1_ref[...]), 0.0)
    x = x + _dot(a * a, w2_ref[...])
    gate = jax.nn.sigmoid(_dot(_rms(x, gple_ref[...]), wpg_ref[...]) + bpg_ref[...])
    x = x + gate * _dot(p_ref[...], wple_ref[...])
    if final_norm:
        x = _rms(x, gfin_ref[...])
    y_ref[...] = x


def _ffn(x2d, p3d, layer, fw, final_norm, cast_src=(), cast_layer=None):
    rows = x2d.shape[0]
    tr = min(FFN_ROWS, rows)
    n_steps = rows // tr
    streams = [_cast_stream(w, cast_layer, n_steps, lambda i: i) for w in cast_src]
    in_specs = [pl.BlockSpec((tr, D_MODEL), lambda i: (i, 0)),
                pl.BlockSpec((None, tr, P_DIM), lambda i: (layer, i, 0))]
    in_specs += [_const_spec(a, l) for a, l in fw] + [s[0] for s in streams]
    return pl.pallas_call(
        functools.partial(_ffn_kernel, n_cast=len(cast_src), final_norm=final_norm),
        grid=(n_steps,),
        in_specs=in_specs,
        out_specs=(pl.BlockSpec((tr, D_MODEL), lambda i: (i, 0)),) + tuple(s[1] for s in streams),
        out_shape=(jax.ShapeDtypeStruct((rows, D_MODEL), F32),) + tuple(s[2] for s in streams),
        compiler_params=pltpu.CompilerParams(
            dimension_semantics=("arbitrary",),
            vmem_limit_bytes=V7X_VMEM_LIMIT_BYTES),
        name="ffn",
    )(x2d, p3d, *[a for a, _ in fw], *cast_src)


def _rows(a):
    return a.reshape(a.shape[0], 1, a.shape[1])


def kernel(x_prompt, x_sample, state_conv, p_prompt, p_sample, g_mix, w_in, b_in, conv_w, conv_b,
           ln_a_g, ln_a_b, w_a_out, ln_v_g, ln_v_b, w_s, b_s, w_b_out, w_o, g_ffn, w_ff1, w_ff2,
           g_ple, w_ple, w_ple_gate, b_ple_gate, g_final):
    depth = w_in.shape[0]
    bsz, seq, _ = x_prompt.shape
    nseq, n_s, _ = x_sample.shape

    ws_pairs = jnp.concatenate([w_s[:, 0::2], w_s[:, 1::2]], axis=3).astype(BF16)
    bs_full = jnp.repeat(jnp.transpose(b_s, (0, 2, 1)), GROUP_DIM, axis=2)
    wmix = jnp.repeat(jnp.transpose(w_s[:, :, :n_s, :n_s], (0, 2, 3, 1)), GROUP_DIM, axis=3)
    w_ple_bf = w_ple.astype(BF16)
    g_fin = g_final.reshape(1, D_MODEL)
    mixer_src = (w_in, w_a_out, w_b_out, w_o)
    ffn_src = (w_ff1, w_ff2, w_ple_gate)
    mixer_bf = tuple(w[0].astype(BF16) for w in mixer_src)

    xp = x_prompt
    xs_t = jnp.transpose(x_sample, (1, 0, 2))
    st_t = jnp.transpose(state_conv, (0, 2, 1, 3))
    p_t = jnp.transpose(p_sample, (0, 2, 1, 3)).reshape(depth, n_s * nseq, P_DIM)
    conv_p, glu_s, v_p, v_s = [], [], [], []
    for i in range(depth):
        last = i == depth - 1
        win_bf, waout_bf, wbout_bf, wo_bf = mixer_bf
        shared = ((_rows(g_mix), i), (win_bf, None), (_rows(b_in), i), (conv_w, i),
                  (_rows(conv_b), i), (_rows(ln_a_g), i), (_rows(ln_a_b), i), (waout_bf, None),
                  (_rows(ln_v_g), i), (_rows(ln_v_b), i))
        tail = ((wbout_bf, None), (wo_bf, None))

        xp, cp, vp, w1_bf, w2_bf, wpg_bf = _mixer_prompt(
            xp, shared + ((ws_pairs, i), (bs_full, i)) + tail, i, ffn_src)
        fw = ((_rows(g_ffn), i), (w1_bf, None), (w2_bf, None), (_rows(g_ple), i),
              (w_ple_bf, i), (wpg_bf, None), (_rows(b_ple_gate), i), (g_fin, None))
        res = _ffn(xp.reshape(bsz * seq, D_MODEL), p_prompt.reshape(depth, bsz * seq, P_DIM), i,
                   fw, last, () if last else mixer_src, None if last else i + 1)
        xp = res[0].reshape(bsz, seq, D_MODEL)

        xs_t, glu_t, v_t = _mixer_sample(
            xs_t, st_t, i, shared + ((wmix, i), (bs_full[:, :n_s], i)) + tail)
        xs_t = _ffn(xs_t.reshape(n_s * nseq, D_MODEL), p_t, i, fw, last)[0].reshape(
            n_s, nseq, D_MODEL)
        mixer_bf = res[1:]

        conv_p.append(cp)
        v_p.append(vp)
        glu_s.append(glu_t)
        v_s.append(v_t)

    y_sample = jnp.transpose(xs_t, (1, 0, 2))
    conv_sample = jnp.concatenate(
        [state_conv[:, :, n_s:], jnp.transpose(jnp.stack(glu_s), (0, 2, 1, 3))], axis=2)
    v_sample = jnp.transpose(jnp.stack(v_s), (0, 2, 1, 3))
    return (xp, y_sample, jnp.stack(conv_p), conv_sample, jnp.stack(v_p), v_sample)
```

```python
import functools

import jax
import jax.numpy as jnp
from jax.experimental import pallas as pl
from jax.experimental.pallas import tpu as pltpu

D_MODEL = 1024
C_A = D_MODEL
C_B = (3 * D_MODEL) // 2
CONV_WIDTH = 31
HIST = CONV_WIDTH - 1
CHUNK = 128
N_GROUPS = 8
GROUP_DIM = C_B // N_GROUPS
PAIR_DIM = 2 * GROUP_DIM
N_PAIRS = N_GROUPS // 2
D_FF = 4 * D_MODEL
P_DIM = 256
EPS = 1e-6

COL_A = 0
COL_B = 2 * C_A
COL_G = 2 * C_A + 2 * C_B
D_IN = 2 * C_A + 2 * C_B + 2 * D_MODEL

LANES = 128
HIST_PAD = 32
TILE_T = 512
FFN_ROWS = 512
V7X_VMEM_LIMIT_BYTES = 56 * 1024 * 1024

BF16 = jnp.bfloat16
F32 = jnp.float32


def _rms(x, g):
    return x * jax.lax.rsqrt(jnp.mean(x * x, axis=-1, keepdims=True) + EPS) * g


def _ln(x, g, b):
    mu = jnp.mean(x, axis=-1, keepdims=True)
    xc = x - mu
    return xc * jax.lax.rsqrt(jnp.mean(xc * xc, axis=-1, keepdims=True) + EPS) * g + b


def _dot(a, b):
    return jnp.dot(a.astype(BF16), b, preferred_element_type=F32)


def _const_spec(arr, layer=None):
    if layer is None:
        shape, index = arr.shape, (0,) * arr.ndim
    else:
        shape, index = (None,) + arr.shape[1:], (layer,) + (0,) * (arr.ndim - 1)
    return pl.BlockSpec(shape, lambda *_: index, pipeline_mode=pl.Buffered(1))


def _cast_stream(stacked, layer, n_steps, step_of):
    _, rows, cols = stacked.shape
    slab = rows // n_steps
    assert slab * n_steps == rows and slab % 16 == 0, (stacked.shape, n_steps)
    in_spec = pl.BlockSpec((None, slab, cols), lambda *g: (layer, step_of(*g), 0))
    out_spec = pl.BlockSpec((slab, cols), lambda *g: (step_of(*g), 0))
    return in_spec, out_spec, jax.ShapeDtypeStruct((rows, cols), BF16)


def _cast_slabs(refs):
    n = len(refs) // 2
    for src, dst in zip(refs[:n], refs[n:]):
        dst[...] = src[...].astype(BF16)


N_MIXER_CONSTS = 14


def _mixer_prompt_kernel(*refs, n_cast):
    x_ref = refs[0]
    (gmix_ref, win_ref, bin_ref, convw_ref, convb_ref, lnag_ref, lnab_ref, waout_ref, lnvg_ref,
     lnvb_ref, ws_ref, bsp_ref, wbout_ref, wo_ref) = refs[1:1 + N_MIXER_CONSTS]
    n_in = 1 + N_MIXER_CONSTS + n_cast
    x1_ref, conv_ref, vlast_ref = refs[n_in:n_in + 3]
    cbuf_ref = refs[-1]
    _cast_slabs(refs[1 + N_MIXER_CONSTS:n_in] + refs[n_in + 3:-1])
    t = pl.program_id(1)
    n_t = pl.num_programs(1)
    tt = x_ref.shape[0]

    @pl.when(t == 0)
    def _():
        cbuf_ref[:, 0:HIST_PAD, :] = jnp.zeros((C_A // LANES, HIST_PAD, LANES), F32)

    x = x_ref[...]
    h = _rms(x, gmix_ref[...]).astype(BF16)

    za = jnp.dot(h, win_ref[:, COL_A:COL_A + 2 * C_A], preferred_element_type=F32)
    za = za + bin_ref[:, COL_A:COL_A + 2 * C_A]
    glu = za[:, :C_A] * jax.nn.sigmoid(za[:, C_A:])
    strips = []
    for l in range(C_A // LANES):
        lanes = slice(l * LANES, (l + 1) * LANES)
        cbuf_ref[l, HIST_PAD:HIST_PAD + tt, :] = glu[:, lanes]
        acc = jnp.zeros((tt, LANES), F32) + convb_ref[:, lanes]
        for k in range(CONV_WIDTH):
            off = HIST_PAD - HIST + k
            acc = acc + convw_ref[k:k + 1, lanes] * cbuf_ref[l, off:off + tt, :]
        cbuf_ref[l, 0:HIST_PAD, :] = cbuf_ref[l, tt:tt + HIST_PAD, :]
        strips.append(acc)
    ya = _ln(jnp.concatenate(strips, axis=1), lnag_ref[...], lnab_ref[...])
    ya = _dot(ya * jax.nn.sigmoid(ya), waout_ref[...])

    zb = jnp.dot(h, win_ref[:, COL_B:COL_B + 2 * C_B], preferred_element_type=F32)
    zb = jax.nn.gelu(zb + bin_ref[:, COL_B:COL_B + 2 * C_B])
    u = zb[:, :C_B]
    v = _ln(zb[:, C_B:], lnvg_ref[...], lnvb_ref[...])
    vb = v.astype(BF16)
    row = jax.lax.broadcasted_iota(jnp.int32, (CHUNK, 2 * CHUNK), 0)
    col = jax.lax.broadcasted_iota(jnp.int32, (CHUNK, 2 * CHUNK), 1)
    causal = (col % CHUNK) <= row
    first_group = jax.lax.broadcasted_iota(jnp.int32, (CHUNK, PAIR_DIM), 1) < GROUP_DIM
    zero = jnp.zeros((CHUNK, PAIR_DIM), BF16)
    mixed_chunks = []
    for c in range(tt // CHUNK):
        pieces = []
        for p in range(N_PAIRS):
            wp = jnp.where(causal, ws_ref[p], jnp.zeros_like(ws_ref[p]))
            vs = vb[c * CHUNK:(c + 1) * CHUNK, p * PAIR_DIM:(p + 1) * PAIR_DIM]
            rhs = jnp.concatenate([jnp.where(first_group, vs, zero),
                                   jnp.where(first_group, zero, vs)], axis=0)
            pieces.append(jnp.dot(wp, rhs, preferred_element_type=F32))
        mixed_chunks.append(jnp.concatenate(pieces, axis=1) + bsp_ref[...])
    mixed = jnp.concatenate(mixed_chunks, axis=0)
    yb = _dot(u * mixed, wbout_ref[...])

    zg = jnp.dot(h, win_ref[:, COL_G:COL_G + 2 * D_MODEL], preferred_element_type=F32)
    zg = zg + bin_ref[:, COL_G:COL_G + 2 * D_MODEL]
    m = jax.nn.sigmoid(zg[:, :D_MODEL]) * ya + jax.nn.sigmoid(zg[:, D_MODEL:]) * yb
    x1_ref[...] = x + _dot(m, wo_ref[...])

    @pl.when(t == n_t - 1)
    def _():
        conv_ref[...] = glu[tt - HIST:, :]
        vlast_ref[...] = v[tt - CHUNK:, :]


def _mixer_prompt(x, lw, layer, cast_src):
    bsz, seq, _ = x.shape
    tt = TILE_T
    n_t = seq // tt
    grid = (bsz, n_t)
    row_spec = pl.BlockSpec((None, tt, D_MODEL), lambda b, t: (b, t, 0))
    streams = [_cast_stream(w, layer, bsz * n_t, lambda b, t: b * n_t + t) for w in cast_src]
    in_specs = [row_spec] + [_const_spec(a, l) for a, l in lw] + [s[0] for s in streams]
    out_shape = (jax.ShapeDtypeStruct((bsz, seq, D_MODEL), F32),
                 jax.ShapeDtypeStruct((bsz, HIST, C_A), F32),
                 jax.ShapeDtypeStruct((bsz, CHUNK, C_B), F32)) + tuple(s[2] for s in streams)
    out_specs = (row_spec,
                 pl.BlockSpec((None, HIST, C_A), lambda b, t: (b, 0, 0)),
                 pl.BlockSpec((None, CHUNK, C_B), lambda b, t: (b, 0, 0))
                 ) + tuple(s[1] for s in streams)
    return pl.pallas_call(
        functools.partial(_mixer_prompt_kernel, n_cast=len(cast_src)),
        grid=grid,
        in_specs=in_specs,
        out_specs=out_specs,
        out_shape=out_shape,
        scratch_shapes=[pltpu.VMEM((C_A // LANES, HIST_PAD + tt, LANES), F32)],
        compiler_params=pltpu.CompilerParams(
            dimension_semantics=("arbitrary", "arbitrary"),
            vmem_limit_bytes=V7X_VMEM_LIMIT_BYTES),
        name="mixer_prompt",
    )(x, *[a for a, _ in lw], *cast_src)


def _mixer_sample_kernel(x_ref, st_ref, gmix_ref, win_ref, bin_ref, convw_ref, convb_ref,
                         lnag_ref, lnab_ref, waout_ref, lnvg_ref, lnvb_ref, wmix_ref, bmix_ref,
                         wbout_ref, wo_ref, x1_ref, glu_ref, v_ref):
    n_s, nb, _ = x_ref.shape
    x = x_ref[...].reshape(n_s * nb, D_MODEL)
    h = _rms(x, gmix_ref[...]).astype(BF16)

    za = jnp.dot(h, win_ref[:, COL_A:COL_A + 2 * C_A], preferred_element_type=F32)
    za = za + bin_ref[:, COL_A:COL_A + 2 * C_A]
    glu = za[:, :C_A] * jax.nn.sigmoid(za[:, C_A:])
    glu_ref[...] = glu.reshape(n_s, nb, C_A)
    conv_rows = []
    for s in range(n_s):
        acc = jnp.zeros((nb, C_A), F32) + convb_ref[...]
        for j in range(s, HIST):
            acc = acc + convw_ref[j - s:j - s + 1, :] * st_ref[j]
        for r in range(s + 1):
            k = HIST - s + r
            acc = acc + convw_ref[k:k + 1, :] * glu[r * nb:(r + 1) * nb, :]
        conv_rows.append(acc)
    ya = _ln(jnp.concatenate(conv_rows, axis=0), lnag_ref[...], lnab_ref[...])
    ya = _dot(ya * jax.nn.sigmoid(ya), waout_ref[...])

    zb = jnp.dot(h, win_ref[:, COL_B:COL_B + 2 * C_B], preferred_element_type=F32)
    zb = jax.nn.gelu(zb + bin_ref[:, COL_B:COL_B + 2 * C_B])
    u = zb[:, :C_B]
    v = _ln(zb[:, C_B:], lnvg_ref[...], lnvb_ref[...])
    v_ref[...] = v.reshape(n_s, nb, C_B)
    mixed_rows = []
    for s in range(n_s):
        acc = jnp.zeros((nb, C_B), F32) + bmix_ref[s:s + 1, :]
        for r in range(s + 1):
            acc = acc + wmix_ref[s, r:r + 1, :] * v[r * nb:(r + 1) * nb, :]
        mixed_rows.append(acc)
    mixed = jnp.concatenate(mixed_rows, axis=0)
    yb = _dot(u * mixed, wbout_ref[...])

    zg = jnp.dot(h, win_ref[:, COL_G:COL_G + 2 * D_MODEL], preferred_element_type=F32)
    zg = zg + bin_ref[:, COL_G:COL_G + 2 * D_MODEL]
    m = jax.nn.sigmoid(zg[:, :D_MODEL]) * ya + jax.nn.sigmoid(zg[:, D_MODEL:]) * yb
    x1_ref[...] = (x + _dot(m, wo_ref[...])).reshape(n_s, nb, D_MODEL)


SAMPLE_SEQS = 64


def _mixer_sample(x_t, st_t, layer, lw):
    n_s, nseq, _ = x_t.shape
    nb = SAMPLE_SEQS
    grid = (nseq // nb,)
    x_spec = pl.BlockSpec((n_s, nb, D_MODEL), lambda i: (0, i, 0))
    in_specs = [x_spec, pl.BlockSpec((None, HIST, nb, C_A), lambda i: (layer, 0, i, 0))]
    in_specs += [_const_spec(a, l) for a, l in lw]
    out_shape = (jax.ShapeDtypeStruct((n_s, nseq, D_MODEL), F32),
                 jax.ShapeDtypeStruct((n_s, nseq, C_A), F32),
                 jax.ShapeDtypeStruct((n_s, nseq, C_B), F32))
    out_specs = (x_spec,
                 pl.BlockSpec((n_s, nb, C_A), lambda i: (0, i, 0)),
                 pl.BlockSpec((n_s, nb, C_B), lambda i: (0, i, 0)))
    return pl.pallas_call(
        _mixer_sample_kernel,
        grid=grid,
        in_specs=in_specs,
        out_specs=out_specs,
        out_shape=out_shape,
        compiler_params=pltpu.CompilerParams(
            dimension_semantics=("arbitrary",),
            vmem_limit_bytes=V7X_VMEM_LIMIT_BYTES),
        name="mixer_sample",
    )(x_t, st_t, *[a for a, _ in lw])


def _shift_conv_state_kernel(st_ref, glu_ref, out_ref):
    n_s = glu_ref.shape[0]
    out_ref[:, 0:HIST - n_s, :] = st_ref[:, n_s:HIST, :]
    for s in range(n_s):
        out_ref[:, HIST - n_s + s, :] = glu_ref[s]


ASSEMBLE_SEQS = 32


def _shift_conv_state(state, glu_t):
    depth, nseq, _, _ = state.shape
    n_s = glu_t.shape[1]
    nb = ASSEMBLE_SEQS
    st_spec = pl.BlockSpec((None, nb, HIST, C_A), lambda l, i: (l, i, 0, 0))
    return pl.pallas_call(
        _shift_conv_state_kernel,
        grid=(depth, nseq // nb),
        in_specs=[st_spec, pl.BlockSpec((None, n_s, nb, C_A), lambda l, i: (l, 0, i, 0))],
        out_specs=st_spec,
        out_shape=jax.ShapeDtypeStruct(state.shape, F32),
        compiler_params=pltpu.CompilerParams(dimension_semantics=("arbitrary", "arbitrary")),
        name="shift_conv_state",
    )(state, glu_t)


N_FFN_CONSTS = 8


def _ffn_kernel(*refs, n_cast, final_norm):
    x_ref, p_ref = refs[:2]
    (gffn_ref, w1_ref, w2_ref, gple_ref, wple_ref, wpg_ref, bpg_ref,
     gfin_ref) = refs[2:2 + N_FFN_CONSTS]
    n_in = 2 + N_FFN_CONSTS + n_cast
    y_ref = refs[n_in]
    _cast_slabs(refs[2 + N_FFN_CONSTS:n_in] + refs[n_in + 1:])
    x = x_ref[...]
    hf = _rms(x, gffn_ref[...])
    a = jnp.maximum(_dot(hf, w1_ref[...]), 0.0)
    x = x + _dot(a * a, w2_ref[...])
    gate = jax.nn.sigmoid(_dot(_rms(x, gple_ref[...]), wpg_ref[...]) + bpg_ref[...])
    x = x + gate * _dot(p_ref[...], wple_ref[...])
    if final_norm:
        x = _rms(x, gfin_ref[...])
    y_ref[...] = x


def _ffn(x2d, p3d, layer, fw, final_norm, cast_src=(), cast_layer=None):
    rows = x2d.shape[0]
    tr = min(FFN_ROWS, rows)
    n_steps = rows // tr
    streams = [_cast_stream(w, cast_layer, n_steps, lambda i: i) for w in cast_src]
    in_specs = [pl.BlockSpec((tr, D_MODEL), lambda i: (i, 0)),
                pl.BlockSpec((None, tr, P_DIM), lambda i: (layer, i, 0))]
    in_specs += [_const_spec(a, l) for a, l in fw] + [s[0] for s in streams]
    return pl.pallas_call(
        functools.partial(_ffn_kernel, n_cast=len(cast_src), final_norm=final_norm),
        grid=(n_steps,),
        in_specs=in_specs,
        out_specs=(pl.BlockSpec((tr, D_MODEL), lambda i: (i, 0)),) + tuple(s[1] for s in streams),
        out_shape=(jax.ShapeDtypeStruct((rows, D_MODEL), F32),) + tuple(s[2] for s in streams),
        compiler_params=pltpu.CompilerParams(
            dimension_semantics=("arbitrary",),
            vmem_limit_bytes=V7X_VMEM_LIMIT_BYTES),
        name="ffn",
    )(x2d, p3d, *[a for a, _ in fw], *cast_src)


def _rows(a):
    return a.reshape(a.shape[0], 1, a.shape[1])


def kernel(x_prompt, x_sample, state_conv, p_prompt, p_sample, g_mix, w_in, b_in, conv_w, conv_b,
           ln_a_g, ln_a_b, w_a_out, ln_v_g, ln_v_b, w_s, b_s, w_b_out, w_o, g_ffn, w_ff1, w_ff2,
           g_ple, w_ple, w_ple_gate, b_ple_gate, g_final):
    depth = w_in.shape[0]
    bsz, seq, _ = x_prompt.shape
    nseq, n_s, _ = x_sample.shape

    ws_pairs = jnp.concatenate([w_s[:, 0::2], w_s[:, 1::2]], axis=3).astype(BF16)
    bs_full = jnp.repeat(jnp.transpose(b_s, (0, 2, 1)), GROUP_DIM, axis=2)
    wmix = jnp.repeat(jnp.transpose(w_s[:, :, :n_s, :n_s], (0, 2, 3, 1)), GROUP_DIM, axis=3)
    w_ple_bf = w_ple.astype(BF16)
    g_fin = g_final.reshape(1, D_MODEL)
    mixer_src = (w_in, w_a_out, w_b_out, w_o)
    ffn_src = (w_ff1, w_ff2, w_ple_gate)
    mixer_bf = tuple(w[0].astype(BF16) for w in mixer_src)

    xp = x_prompt
    xs_t = jnp.transpose(x_sample, (1, 0, 2))
    st_t = jnp.transpose(state_conv, (0, 2, 1, 3))
    p_t = jnp.transpose(p_sample, (0, 2, 1, 3)).reshape(depth, n_s * nseq, P_DIM)
    conv_p, glu_s, v_p, v_s = [], [], [], []
    for i in range(depth):
        last = i == depth - 1
        win_bf, waout_bf, wbout_bf, wo_bf = mixer_bf
        shared = ((_rows(g_mix), i), (win_bf, None), (_rows(b_in), i), (conv_w, i),
                  (_rows(conv_b), i), (_rows(ln_a_g), i), (_rows(ln_a_b), i), (waout_bf, None),
                  (_rows(ln_v_g), i), (_rows(ln_v_b), i))
        tail = ((wbout_bf, None), (wo_bf, None))

        xp, cp, vp, w1_bf, w2_bf, wpg_bf = _mixer_prompt(
            xp, shared + ((ws_pairs, i), (bs_full, i)) + tail, i, ffn_src)
        fw = ((_rows(g_ffn), i), (w1_bf, None), (w2_bf, None), (_rows(g_ple), i),
              (w_ple_bf, i), (wpg_bf, None), (_rows(b_ple_gate), i), (g_fin, None))
        res = _ffn(xp.reshape(bsz * seq, D_MODEL), p_prompt.reshape(depth, bsz * seq, P_DIM), i,
                   fw, last, () if last else mixer_src, None if last else i + 1)
        xp = res[0].reshape(bsz, seq, D_MODEL)

        xs_t, glu_t, v_t = _mixer_sample(
            xs_t, st_t, i, shared + ((wmix, i), (bs_full[:, :n_s], i)) + tail)
        xs_t = _ffn(xs_t.reshape(n_s * nseq, D_MODEL), p_t, i, fw, last)[0].reshape(
            n_s, nseq, D_MODEL)
        mixer_bf = res[1:]

        conv_p.append(cp)
        v_p.append(vp)
        glu_s.append(glu_t)
        v_s.append(v_t)

    y_sample = jnp.transpose(xs_t, (1, 0, 2))
    conv_sample = _shift_conv_state(state_conv, jnp.stack(glu_s))
    v_sample = jnp.transpose(jnp.stack(v_s), (0, 2, 1, 3))
    return (xp, y_sample, jnp.stack(conv_p), conv_sample, jnp.stack(v_p), v_sample)
```

```python
import functools

import jax
import jax.numpy as jnp
from jax.experimental import pallas as pl
from jax.experimental.pallas import tpu as pltpu

D_MODEL = 1024
C_A = D_MODEL
C_B = (3 * D_MODEL) // 2
CONV_WIDTH = 31
HIST = CONV_WIDTH - 1
CHUNK = 128
N_GROUPS = 8
GROUP_DIM = C_B // N_GROUPS
PAIR_DIM = 2 * GROUP_DIM
N_PAIRS = N_GROUPS // 2
D_FF = 4 * D_MODEL
P_DIM = 256
EPS = 1e-6

COL_A = 0
COL_B = 2 * C_A
COL_G = 2 * C_A + 2 * C_B
D_IN = 2 * C_A + 2 * C_B + 2 * D_MODEL

LANES = 128
HIST_PAD = 32
TILE_T = 512
FFN_ROWS = 512
V7X_VMEM_LIMIT_BYTES = 56 * 1024 * 1024

BF16 = jnp.bfloat16
F32 = jnp.float32


def _rms(x, g):
    return x * jax.lax.rsqrt(jnp.mean(x * x, axis=-1, keepdims=True) + EPS) * g


def _ln(x, g, b):
    mu = jnp.mean(x, axis=-1, keepdims=True)
    xc = x - mu
    return xc * jax.lax.rsqrt(jnp.mean(xc * xc, axis=-1, keepdims=True) + EPS) * g + b


def _dot(a, b):
    return jnp.dot(a.astype(BF16), b, preferred_element_type=F32)


def _const_spec(arr, layer=None):
    if layer is None:
        shape, index = arr.shape, (0,) * arr.ndim
    else:
        shape, index = (None,) + arr.shape[1:], (layer,) + (0,) * (arr.ndim - 1)
    return pl.BlockSpec(shape, lambda *_: index, pipeline_mode=pl.Buffered(1))


def _cast_stream(stacked, layer, n_steps, step_of):
    _, rows, cols = stacked.shape
    slab = rows // n_steps
    assert slab * n_steps == rows and slab % 16 == 0, (stacked.shape, n_steps)
    in_spec = pl.BlockSpec((None, slab, cols), lambda *g: (layer, step_of(*g), 0))
    out_spec = pl.BlockSpec((slab, cols), lambda *g: (step_of(*g), 0))
    return in_spec, out_spec, jax.ShapeDtypeStruct((rows, cols), BF16)


def _cast_slabs(refs):
    n = len(refs) // 2
    for src, dst in zip(refs[:n], refs[n:]):
        dst[...] = src[...].astype(BF16)


N_MIXER_CONSTS = 14


def _mixer_prompt_kernel(*refs, n_cast):
    x_ref = refs[0]
    (gmix_ref, win_ref, bin_ref, convw_ref, convb_ref, lnag_ref, lnab_ref, waout_ref, lnvg_ref,
     lnvb_ref, ws_ref, bsp_ref, wbout_ref, wo_ref) = refs[1:1 + N_MIXER_CONSTS]
    n_in = 1 + N_MIXER_CONSTS + n_cast
    x1_ref, conv_ref, vlast_ref = refs[n_in:n_in + 3]
    cbuf_ref = refs[-1]
    _cast_slabs(refs[1 + N_MIXER_CONSTS:n_in] + refs[n_in + 3:-1])
    t = pl.program_id(1)
    n_t = pl.num_programs(1)
    tt = x_ref.shape[0]

    @pl.when(t == 0)
    def _():
        cbuf_ref[:, 0:HIST_PAD, :] = jnp.zeros((C_A // LANES, HIST_PAD, LANES), F32)

    x = x_ref[...]
    h = _rms(x, gmix_ref[...]).astype(BF16)

    za = jnp.dot(h, win_ref[:, COL_A:COL_A + 2 * C_A], preferred_element_type=F32)
    za = za + bin_ref[:, COL_A:COL_A + 2 * C_A]
    glu = za[:, :C_A] * jax.nn.sigmoid(za[:, C_A:])
    zg = jnp.dot(h, win_ref[:, COL_G:COL_G + 2 * D_MODEL], preferred_element_type=F32)
    zg = zg + bin_ref[:, COL_G:COL_G + 2 * D_MODEL]
    strips, gate_a, gate_b = [], [], []
    for l in range(C_A // LANES):
        lanes = slice(l * LANES, (l + 1) * LANES)
        cbuf_ref[l, HIST_PAD:HIST_PAD + tt, :] = glu[:, lanes]
        acc = jnp.zeros((tt, LANES), F32) + convb_ref[:, lanes]
        for k in range(CONV_WIDTH):
            off = HIST_PAD - HIST + k
            acc = acc + convw_ref[k:k + 1, lanes] * cbuf_ref[l, off:off + tt, :]
        cbuf_ref[l, 0:HIST_PAD, :] = cbuf_ref[l, tt:tt + HIST_PAD, :]
        strips.append(acc)
        gate_a.append(jax.nn.sigmoid(zg[:, l * LANES:(l + 1) * LANES]))
        gate_b.append(jax.nn.sigmoid(zg[:, D_MODEL + l * LANES:D_MODEL + (l + 1) * LANES]))
    ya = _ln(jnp.concatenate(strips, axis=1), lnag_ref[...], lnab_ref[...])
    ya = _dot(ya * jax.nn.sigmoid(ya), waout_ref[...])

    zb = jnp.dot(h, win_ref[:, COL_B:COL_B + 2 * C_B], preferred_element_type=F32)
    zb = jax.nn.gelu(zb + bin_ref[:, COL_B:COL_B + 2 * C_B])
    u = zb[:, :C_B]
    v = _ln(zb[:, C_B:], lnvg_ref[...], lnvb_ref[...])
    vb = v.astype(BF16)
    row = jax.lax.broadcasted_iota(jnp.int32, (CHUNK, 2 * CHUNK), 0)
    col = jax.lax.broadcasted_iota(jnp.int32, (CHUNK, 2 * CHUNK), 1)
    causal = (col % CHUNK) <= row
    first_group = jax.lax.broadcasted_iota(jnp.int32, (CHUNK, PAIR_DIM), 1) < GROUP_DIM
    zero = jnp.zeros((CHUNK, PAIR_DIM), BF16)
    mixed_chunks = []
    for c in range(tt // CHUNK):
        pieces = []
        for p in range(N_PAIRS):
            wp = jnp.where(causal, ws_ref[p], jnp.zeros_like(ws_ref[p]))
            vs = vb[c * CHUNK:(c + 1) * CHUNK, p * PAIR_DIM:(p + 1) * PAIR_DIM]
            rhs = jnp.concatenate([jnp.where(first_group, vs, zero),
                                   jnp.where(first_group, zero, vs)], axis=0)
            pieces.append(jnp.dot(wp, rhs, preferred_element_type=F32))
        mixed_chunks.append(jnp.concatenate(pieces, axis=1) + bsp_ref[...])
    mixed = jnp.concatenate(mixed_chunks, axis=0)
    yb = _dot(u * mixed, wbout_ref[...])

    m = jnp.concatenate(gate_a, axis=1) * ya + jnp.concatenate(gate_b, axis=1) * yb
    x1_ref[...] = x + _dot(m, wo_ref[...])

    @pl.when(t == n_t - 1)
    def _():
        conv_ref[...] = glu[tt - HIST:, :]
        vlast_ref[...] = v[tt - CHUNK:, :]


def _mixer_prompt(x, lw, layer, cast_src):
    bsz, seq, _ = x.shape
    tt = TILE_T
    n_t = seq // tt
    grid = (bsz, n_t)
    row_spec = pl.BlockSpec((None, tt, D_MODEL), lambda b, t: (b, t, 0))
    streams = [_cast_stream(w, layer, bsz * n_t, lambda b, t: b * n_t + t) for w in cast_src]
    in_specs = [row_spec] + [_const_spec(a, l) for a, l in lw] + [s[0] for s in streams]
    out_shape = (jax.ShapeDtypeStruct((bsz, seq, D_MODEL), F32),
                 jax.ShapeDtypeStruct((bsz, HIST, C_A), F32),
                 jax.ShapeDtypeStruct((bsz, CHUNK, C_B), F32)) + tuple(s[2] for s in streams)
    out_specs = (row_spec,
                 pl.BlockSpec((None, HIST, C_A), lambda b, t: (b, 0, 0)),
                 pl.BlockSpec((None, CHUNK, C_B), lambda b, t: (b, 0, 0))
                 ) + tuple(s[1] for s in streams)
    return pl.pallas_call(
        functools.partial(_mixer_prompt_kernel, n_cast=len(cast_src)),
        grid=grid,
        in_specs=in_specs,
        out_specs=out_specs,
        out_shape=out_shape,
        scratch_shapes=[pltpu.VMEM((C_A // LANES, HIST_PAD + tt, LANES), F32)],
        compiler_params=pltpu.CompilerParams(
            dimension_semantics=("arbitrary", "arbitrary"),
            vmem_limit_bytes=V7X_VMEM_LIMIT_BYTES),
        name="mixer_prompt",
    )(x, *[a for a, _ in lw], *cast_src)


def _mixer_sample_kernel(x_ref, st_ref, gmix_ref, win_ref, bin_ref, convw_ref, convb_ref,
                         lnag_ref, lnab_ref, waout_ref, lnvg_ref, lnvb_ref, wmix_ref, bmix_ref,
                         wbout_ref, wo_ref, x1_ref, glu_ref, v_ref):
    n_s, nb, _ = x_ref.shape
    x = x_ref[...].reshape(n_s * nb, D_MODEL)
    h = _rms(x, gmix_ref[...]).astype(BF16)

    za = jnp.dot(h, win_ref[:, COL_A:COL_A + 2 * C_A], preferred_element_type=F32)
    za = za + bin_ref[:, COL_A:COL_A + 2 * C_A]
    glu = za[:, :C_A] * jax.nn.sigmoid(za[:, C_A:])
    glu_ref[...] = glu.reshape(n_s, nb, C_A)
    conv_rows = []
    for s in range(n_s):
        acc = jnp.zeros((nb, C_A), F32) + convb_ref[...]
        for j in range(s, HIST):
            acc = acc + convw_ref[j - s:j - s + 1, :] * st_ref[j]
        for r in range(s + 1):
            k = HIST - s + r
            acc = acc + convw_ref[k:k + 1, :] * glu[r * nb:(r + 1) * nb, :]
        conv_rows.append(acc)
    ya = _ln(jnp.concatenate(conv_rows, axis=0), lnag_ref[...], lnab_ref[...])
    ya = _dot(ya * jax.nn.sigmoid(ya), waout_ref[...])

    zb = jnp.dot(h, win_ref[:, COL_B:COL_B + 2 * C_B], preferred_element_type=F32)
    zb = jax.nn.gelu(zb + bin_ref[:, COL_B:COL_B + 2 * C_B])
    u = zb[:, :C_B]
    v = _ln(zb[:, C_B:], lnvg_ref[...], lnvb_ref[...])
    v_ref[...] = v.reshape(n_s, nb, C_B)
    mixed_rows = []
    for s in range(n_s):
        acc = jnp.zeros((nb, C_B), F32) + bmix_ref[s:s + 1, :]
        for r in range(s + 1):
            acc = acc + wmix_ref[s, r:r + 1, :] * v[r * nb:(r + 1) * nb, :]
        mixed_rows.append(acc)
    mixed = jnp.concatenate(mixed_rows, axis=0)
    yb = _dot(u * mixed, wbout_ref[...])

    zg = jnp.dot(h, win_ref[:, COL_G:COL_G + 2 * D_MODEL], preferred_element_type=F32)
    zg = zg + bin_ref[:, COL_G:COL_G + 2 * D_MODEL]
    m = jax.nn.sigmoid(zg[:, :D_MODEL]) * ya + jax.nn.sigmoid(zg[:, D_MODEL:]) * yb
    x1_ref[...] = (x + _dot(m, wo_ref[...])).reshape(n_s, nb, D_MODEL)


SAMPLE_SEQS = 64


def _mixer_sample(x_t, st_t, layer, lw):
    n_s, nseq, _ = x_t.shape
    nb = SAMPLE_SEQS
    grid = (nseq // nb,)
    x_spec = pl.BlockSpec((n_s, nb, D_MODEL), lambda i: (0, i, 0))
    in_specs = [x_spec, pl.BlockSpec((None, HIST, nb, C_A), lambda i: (layer, 0, i, 0))]
    in_specs += [_const_spec(a, l) for a, l in lw]
    out_shape = (jax.ShapeDtypeStruct((n_s, nseq, D_MODEL), F32),
                 jax.ShapeDtypeStruct((n_s, nseq, C_A), F32),
                 jax.ShapeDtypeStruct((n_s, nseq, C_B), F32))
    out_specs = (x_spec,
                 pl.BlockSpec((n_s, nb, C_A), lambda i: (0, i, 0)),
                 pl.BlockSpec((n_s, nb, C_B), lambda i: (0, i, 0)))
    return pl.pallas_call(
        _mixer_sample_kernel,
        grid=grid,
        in_specs=in_specs,
        out_specs=out_specs,
        out_shape=out_shape,
        compiler_params=pltpu.CompilerParams(
            dimension_semantics=("arbitrary",),
            vmem_limit_bytes=V7X_VMEM_LIMIT_BYTES),
        name="mixer_sample",
    )(x_t, st_t, *[a for a, _ in lw])


N_FFN_CONSTS = 8


def _ffn_kernel(*refs, n_cast, final_norm):
    x_ref, p_ref = refs[:2]
    (gffn_ref, w1_ref, w2_ref, gple_ref, wple_ref, wpg_ref, bpg_ref,
     gfin_ref) = refs[2:2 + N_FFN_CONSTS]
    n_in = 2 + N_FFN_CONSTS + n_cast
    y_ref = refs[n_in]
    _cast_slabs(refs[2 + N_FFN_CONSTS:n_in] + refs[n_in + 1:])
    x = x_ref[...]
    hf = _rms(x, gffn_ref[...])
    a = jnp.maximum(_dot(hf, w1_ref[...]), 0.0)
    x = x + _dot(a * a, w2_ref[...])
    gate = jax.nn.sigmoid(_dot(_rms(x, gple_ref[...]), wpg_ref[...]) + bpg_ref[...])
    x = x + gate * _dot(p_ref[...], wple_ref[...])
    if final_norm:
        x = _rms(x, gfin_ref[...])
    y_ref[...] = x


def _ffn(x2d, p3d, layer, fw, final_norm, cast_src=(), cast_layer=None):
    rows = x2d.shape[0]
    tr = min(FFN_ROWS, rows)
    n_steps = rows // tr
    streams = [_cast_stream(w, cast_layer, n_steps, lambda i: i) for w in cast_src]
    in_specs = [pl.BlockSpec((tr, D_MODEL), lambda i: (i, 0)),
                pl.BlockSpec((None, tr, P_DIM), lambda i: (layer, i, 0))]
    in_specs += [_const_spec(a, l) for a, l in fw] + [s[0] for s in streams]
    return pl.pallas_call(
        functools.partial(_ffn_kernel, n_cast=len(cast_src), final_norm=final_norm),
        grid=(n_steps,),
        in_specs=in_specs,
        out_specs=(pl.BlockSpec((tr, D_MODEL), lambda i: (i, 0)),) + tuple(s[1] for s in streams),
        out_shape=(jax.ShapeDtypeStruct((rows, D_MODEL), F32),) + tuple(s[2] for s in streams),
        compiler_params=pltpu.CompilerParams(
            dimension_semantics=("arbitrary",),
            vmem_limit_bytes=V7X_VMEM_LIMIT_BYTES),
        name="ffn",
    )(x2d, p3d, *[a for a, _ in fw], *cast_src)


def _rows(a):
    return a.reshape(a.shape[0], 1, a.shape[1])


def kernel(x_prompt, x_sample, state_conv, p_prompt, p_sample, g_mix, w_in, b_in, conv_w, conv_b,
           ln_a_g, ln_a_b, w_a_out, ln_v_g, ln_v_b, w_s, b_s, w_b_out, w_o, g_ffn, w_ff1, w_ff2,
           g_ple, w_ple, w_ple_gate, b_ple_gate, g_final):
    depth = w_in.shape[0]
    bsz, seq, _ = x_prompt.shape
    nseq, n_s, _ = x_sample.shape

    ws_pairs = jnp.concatenate([w_s[:, 0::2], w_s[:, 1::2]], axis=3).astype(BF16)
    bs_full = jnp.repeat(jnp.transpose(b_s, (0, 2, 1)), GROUP_DIM, axis=2)
    wmix = jnp.repeat(jnp.transpose(w_s[:, :, :n_s, :n_s], (0, 2, 3, 1)), GROUP_DIM, axis=3)
    w_ple_bf = w_ple.astype(BF16)
    g_fin = g_final.reshape(1, D_MODEL)
    mixer_src = (w_in, w_a_out, w_b_out, w_o)
    ffn_src = (w_ff1, w_ff2, w_ple_gate)
    mixer_bf = tuple(w[0].astype(BF16) for w in mixer_src)

    xp = x_prompt
    xs_t = jnp.transpose(x_sample, (1, 0, 2))
    st_t = jnp.transpose(state_conv, (0, 2, 1, 3))
    p_t = jnp.transpose(p_sample, (0, 2, 1, 3)).reshape(depth, n_s * nseq, P_DIM)
    conv_p, glu_s, v_p, v_s = [], [], [], []
    for i in range(depth):
        last = i == depth - 1
        win_bf, waout_bf, wbout_bf, wo_bf = mixer_bf
        shared = ((_rows(g_mix), i), (win_bf, None), (_rows(b_in), i), (conv_w, i),
                  (_rows(conv_b), i), (_rows(ln_a_g), i), (_rows(ln_a_b), i), (waout_bf, None),
                  (_rows(ln_v_g), i), (_rows(ln_v_b), i))
        tail = ((wbout_bf, None), (wo_bf, None))

        xp, cp, vp, w1_bf, w2_bf, wpg_bf = _mixer_prompt(
            xp, shared + ((ws_pairs, i), (bs_full, i)) + tail, i, ffn_src)
        fw = ((_rows(g_ffn), i), (w1_bf, None), (w2_bf, None), (_rows(g_ple), i),
              (w_ple_bf, i), (wpg_bf, None), (_rows(b_ple_gate), i), (g_fin, None))
        res = _ffn(xp.reshape(bsz * seq, D_MODEL), p_prompt.reshape(depth, bsz * seq, P_DIM), i,
                   fw, last, () if last else mixer_src, None if last else i + 1)
        xp = res[0].reshape(bsz, seq, D_MODEL)

        xs_t, glu_t, v_t = _mixer_sample(
            xs_t, st_t, i, shared + ((wmix, i), (bs_full[:, :n_s], i)) + tail)
        xs_t = _ffn(xs_t.reshape(n_s * nseq, D_MODEL), p_t, i, fw, last)[0].reshape(
            n_s, nseq, D_MODEL)
        mixer_bf = res[1:]

        conv_p.append(cp)
        v_p.append(vp)
        glu_s.append(glu_t)
        v_s.append(v_t)

    y_sample = jnp.transpose(xs_t, (1, 0, 2))
    conv_sample = jnp.concatenate(
        [state_conv[:, :, n_s:], jnp.transpose(jnp.stack(glu_s), (0, 2, 1, 3))], axis=2)
    v_sample = jnp.transpose(jnp.stack(v_s), (0, 2, 1, 3))
    return (xp, y_sample, jnp.stack(conv_p), conv_sample, jnp.stack(v_p), v_sample)
```
